```python
import jax, jax.numpy as jnp
from jax import lax
import numpy as np

D_MODEL = 1024
BATCH = 16
SEQ = 256
DEPTH = 1
DEC_BATCH = 8
DEC_SEQ = 4096
PAST_LEN = 256

GRID_W = 64
N_FOURIER_GROUPS = 4
FOURIER_GROUP_DIM = 128
D_FOURIER = N_FOURIER_GROUPS * FOURIER_GROUP_DIM
RWKV_HEAD_DIM = 64
D_RWKV = D_MODEL
N_RWKV_HEADS = D_RWKV // RWKV_HEAD_DIM
LORA_W = 64
LORA_A = 64
LORA_G = 128
N_DIR = 2
SHORT_CONV = 3
GN_EPS = 64e-5
N_EXPERTS = 16
CAPACITY_FACTOR = 2
D_EXPERT = 2048
LN_EPS = 1e-5
DEEPNORM_ALPHA = (2.0 * DEPTH) ** 0.25
DEEPNORM_BETA = (8.0 * DEPTH) ** -0.25
IN_WIDTHS = (D_FOURIER, 3 * D_RWKV, N_DIR * LORA_W, N_DIR * LORA_A, LORA_G, D_MODEL, D_MODEL)
D_IN = sum(IN_WIDTHS)
SPLIT_POINTS = tuple(sum(IN_WIDTHS[:i + 1]) for i in range(len(IN_WIDTHS) - 1))

kernel_name = "fnet_rwkv7_ec_moe_diffusion_step"


def layer_norm(x, g, b):
    xf = x.astype(jnp.float32)
    mu = jnp.mean(xf, axis=-1, keepdims=True)
    var = jnp.mean(jnp.square(xf - mu), axis=-1, keepdims=True)
    return ((xf - mu) * lax.rsqrt(var + LN_EPS) * g + b).astype(x.dtype)


def to_heads(t):
    return t.reshape(*t.shape[:-1], N_RWKV_HEADS, RWKV_HEAD_DIM)


def centred_conv(u, w):
    t = u.shape[1]
    pad = SHORT_CONV // 2
    up = jnp.pad(u, ((0, 0), (pad, pad), (0, 0)))
    out = up[:, 0:t] * w[0]
    for i in range(1, SHORT_CONV):
        out = out + up[:, i:i + t] * w[i]
    return out


def fourier_mix(u):
    b, t, _ = u.shape
    ug = u.astype(jnp.float32).reshape(b, t, N_FOURIER_GROUPS, FOURIER_GROUP_DIM)
    f = jnp.fft.fft2(ug, axes=(1, 3), norm="ortho").real
    return f.reshape(b, t, D_FOURIER).astype(u.dtype)


def wkv7_scan(r, w, k, v, kk, a, s0, reverse):
    def step(S, inp):
        r_t, w_t, k_t, v_t, kk_t, a_t = inp
        sa = jnp.einsum('bhij,bhj->bhi', S, -kk_t)
        S = (S * w_t[:, :, None, :] + sa[..., None] * (kk_t * a_t)[:, :, None, :]
             + v_t[..., None] * k_t[:, :, None, :])
        y = jnp.einsum('bhij,bhj->bhi', S, r_t)
        return S, y
    xs = tuple(jnp.swapaxes(t, 0, 1) for t in (r, w, k, v, kk, a))
    s_final, ys = lax.scan(step, s0, xs, reverse=reverse)
    return jnp.swapaxes(ys, 0, 1), s_final


def rwkv7_direction(r, k, v, kk, decay, a_rate, k_a, r_k, s0, reverse):
    k_d = to_heads(k * (1.0 + (a_rate - 1.0) * k_a))
    y, s_final = wkv7_scan(r, to_heads(decay), k_d, v, kk, to_heads(a_rate),
                           s0.astype(jnp.float32), reverse)
    bonus = jnp.sum(r * k_d * r_k, axis=-1, keepdims=True) * v
    return y, bonus, s_final


def parallel_mixer(h, s_f0, s_b0, w_in, conv_w, w_decay0, w_decay_up, a0, a_up, g_up,
                   k_k, k_a, r_k, gn_g, gn_b, w_fo, w_ro, w_out):
    b, t, _ = h.shape
    proj = h @ w_in
    u_f, rkv, w_dn, a_dn, g_dn, gate_f, gate_r = jnp.split(proj, SPLIT_POINTS, axis=-1)
    o_f = fourier_mix(u_f) @ w_fo
    rkv = centred_conv(rkv, conv_w).astype(jnp.float32)
    r, k, v = jnp.split(rkv, 3, axis=-1)
    w_dn = w_dn.reshape(b, t, N_DIR, LORA_W).astype(jnp.float32)
    a_dn = a_dn.reshape(b, t, N_DIR, LORA_A).astype(jnp.float32)
    w_log = -jax.nn.softplus(-(w_decay0 + jnp.einsum('btdr,drc->btdc', jnp.tanh(w_dn), w_decay_up))) - 0.5
    decay = jnp.exp(-jnp.exp(w_log))
    a_rate = jax.nn.sigmoid(a0 + jnp.einsum('btdr,drc->btdc', a_dn, a_up))
    g = jax.nn.sigmoid(g_dn) @ g_up
    kk = to_heads(k * k_k)
    kk = kk * lax.rsqrt(jnp.sum(jnp.square(kk), axis=-1, keepdims=True) + 1e-12)
    r_h, v_h = to_heads(r), to_heads(v)
    y_f, bonus_f, s_f = rwkv7_direction(r_h, k, v_h, kk, decay[:, :, 0], a_rate[:, :, 0], k_a, r_k, s_f0, False)
    y_b, bonus_b, s_b = rwkv7_direction(r_h, k, v_h, kk, decay[:, :, 1], a_rate[:, :, 1], k_a, r_k, s_b0, True)
    y = y_f + y_b
    mu = jnp.mean(y, axis=-1, keepdims=True)
    var = jnp.mean(jnp.square(y - mu), axis=-1, keepdims=True)
    y = ((y - mu) * lax.rsqrt(var + GN_EPS)).reshape(b, t, D_RWKV) * gn_g + gn_b
    y = y + (bonus_f + bonus_b).reshape(b, t, D_RWKV)
    o_r = (y * g).astype(h.dtype) @ w_ro
    merged = jax.nn.sigmoid(gate_f) * o_f + jax.nn.sigmoid(gate_r) * o_r
    return merged @ w_out, s_f, s_b


def expert_choice_ffn(h, w_router, w_e1, w_e3, w_e2):
    b, t, d = h.shape
    n = b * t
    cap = CAPACITY_FACTOR * n // N_EXPERTS
    hf = h.reshape(n, d)
    aff = jax.nn.softmax((hf @ w_router).astype(jnp.float32), axis=-1)
    gate, idx = lax.top_k(aff.T, cap)
    xe = hf[idx]
    he = jax.nn.silu(jnp.einsum('ecd,edf->ecf', xe, w_e1)) * jnp.einsum('ecd,edf->ecf', xe, w_e3)
    ye = jnp.einsum('ecf,efd->ecd', he, w_e2) * gate[..., None].astype(h.dtype)
    out = jnp.zeros_like(hf).at[idx.reshape(-1)].add(ye.reshape(-1, d))
    return out.reshape(b, t, d)


def trunk_layer(x, cond, s_f0, s_b0, w_ada, b_ada, w_in, conv_w, w_decay0, w_decay_up, a0, a_up,
                g_up, k_k, k_a, r_k, gn_g, gn_b, w_fo, w_ro, w_out, ln1_g, ln1_b,
                w_router, w_e1, w_e3, w_e2, ln2_g, ln2_b):
    mod = (jax.nn.silu(cond) @ w_ada + b_ada)[:, None, :]
    sh1, sc1, g1, sh2, sc2, g2 = jnp.split(mod, 6, axis=-1)
    h = x * (1.0 + sc1) + sh1
    mix, s_f, s_b = parallel_mixer(h, s_f0, s_b0, w_in, conv_w, w_decay0, w_decay_up, a0, a_up, g_up,
                                   k_k, k_a, r_k, gn_g, gn_b, w_fo, w_ro, w_out)
    x = layer_norm(DEEPNORM_ALPHA * x + g1 * mix, ln1_g, ln1_b)
    h2 = x * (1.0 + sc2) + sh2
    x = layer_norm(DEEPNORM_ALPHA * x + g2 * expert_choice_ffn(h2, w_router, w_e1, w_e3, w_e2), ln2_g, ln2_b)
    return x, s_f, s_b


def setup_inputs(seed: int = 0) -> dict:
    key = jax.random.key(seed)
    ks = iter(jax.random.split(key, 40))
    nrm = lambda shape, s: jax.random.normal(next(ks), shape, jnp.float32) * s
    D, R, H, N, E, L = D_MODEL, D_RWKV, N_RWKV_HEADS, RWKV_HEAD_DIM, N_EXPERTS, DEPTH
    conv_center = jnp.zeros((SHORT_CONV, 1), jnp.float32).at[SHORT_CONV // 2].set(1.0)
    return {
        "x_prompt": nrm((BATCH, SEQ, D), 1.0),
        "x_sample": nrm((DEC_BATCH, DEC_SEQ, D), 1.0),
        "state_fwd": nrm((DEC_BATCH, L, H, N, N), 0.5),
        "state_bwd": nrm((DEC_BATCH, L, H, N, N), 0.5),
        "c": nrm((DEC_BATCH, D), 1.0),
        "c_ctx": nrm((D,), 1.0),
        "w_ada": nrm((L, D, 6 * D), 0.5 * D ** -0.5),
        "b_ada": nrm((L, 6 * D), 0.02),
        "w_in": nrm((L, D, D_IN), D ** -0.5),
        "conv_w": conv_center + nrm((L, SHORT_CONV, 3 * R), 0.2),
        "w_decay0": jax.random.uniform(next(ks), (L, N_DIR, R), jnp.float32, -6.0, -1.0),
        "w_decay_up": nrm((L, N_DIR, LORA_W, R), 0.1 * LORA_W ** -0.5),
        "a0": nrm((L, N_DIR, R), 0.1),
        "a_up": nrm((L, N_DIR, LORA_A, R), 0.1 * LORA_A ** -0.5),
        "g_up": nrm((L, LORA_G, R), LORA_G ** -0.5),
        "k_k": 0.85 + nrm((L, R), 0.05),
        "k_a": 1.0 + nrm((L, R), 0.05),
        "r_k": nrm((L, H, N), 0.1),
        "gn_g": 1.0 + nrm((L, R), 0.05),
        "gn_b": nrm((L, R), 0.02),
        "w_fo": nrm((L, D_FOURIER, D), D_FOURIER ** -0.5),
        "w_ro": nrm((L, R, D), R ** -0.5),
        "w_out": nrm((L, D, D), DEEPNORM_BETA * D ** -0.5),
        "ln1_g": 1.0 + nrm((L, D), 0.05),
        "ln1_b": nrm((L, D), 0.02),
        "w_router": nrm((L, D, E), D ** -0.5),
        "w_e1": nrm((L, E, D, D_EXPERT), D ** -0.5),
        "w_e3": nrm((L, E, D, D_EXPERT), D ** -0.5),
        "w_e2": nrm((L, E, D_EXPERT, D), DEEPNORM_BETA * D_EXPERT ** -0.5),
        "ln2_g": 1.0 + nrm((L, D), 0.05),
        "ln2_b": nrm((L, D), 0.02),
    }


def reference(x_prompt, x_sample, state_fwd, state_bwd, c, c_ctx, w_ada, b_ada, w_in, conv_w,
              w_decay0, w_decay_up, a0, a_up, g_up, k_k, k_a, r_k, gn_g, gn_b, w_fo, w_ro, w_out,
              ln1_g, ln1_b, w_router, w_e1, w_e3, w_e2, ln2_g, ln2_b):
    n_ctx_req = x_prompt.shape[0]
    cond_ctx = jnp.broadcast_to(c_ctx, (n_ctx_req, D_MODEL))
    zero_state = jnp.zeros((n_ctx_req, N_RWKV_HEADS, RWKV_HEAD_DIM, RWKV_HEAD_DIM), jnp.float32)
    xp, xs = x_prompt, x_sample
    new_f, new_b = [], []
    for l in range(DEPTH):
        lp = (w_ada[l], b_ada[l], w_in[l], conv_w[l], w_decay0[l], w_decay_up[l], a0[l], a_up[l],
              g_up[l], k_k[l], k_a[l], r_k[l], gn_g[l], gn_b[l], w_fo[l], w_ro[l], w_out[l],
              ln1_g[l], ln1_b[l], w_router[l], w_e1[l], w_e3[l], w_e2[l], ln2_g[l], ln2_b[l])
        xp, s_f, s_b = trunk_layer(xp, cond_ctx, zero_state, zero_state, *lp)
        new_f.append(s_f)
        new_b.append(s_b)
        xs, _, _ = trunk_layer(xs, c, state_fwd[:, l], state_bwd[:, l], *lp)
    new_state_fwd = jnp.stack(new_f, axis=1).astype(x_prompt.dtype)
    new_state_bwd = jnp.stack(new_b, axis=1).astype(x_prompt.dtype)
    return (xp, xs, new_state_fwd, new_state_bwd)
```

```python
import functools

import numpy as np
import jax
import jax.numpy as jnp
from jax import lax
from jax.experimental import pallas as pl
from jax.experimental.pallas import tpu as pltpu

D_MODEL = 1024
N_FOURIER_GROUPS = 4
FOURIER_GROUP_DIM = 128
D_FOURIER = N_FOURIER_GROUPS * FOURIER_GROUP_DIM
HEAD_DIM = 64
D_RWKV = D_MODEL
N_HEADS = D_RWKV // HEAD_DIM
LORA_W = 64
LORA_A = 64
LORA_G = 128
N_DIR = 2
SHORT_CONV = 3
GN_EPS = 64e-5
N_EXPERTS = 16
CAPACITY_FACTOR = 2
D_EXPERT = 2048
LN_EPS = 1e-5
DEPTH = 1
DEEPNORM_ALPHA = (2.0 * DEPTH) ** 0.25
IN_WIDTHS = (D_FOURIER, 3 * D_RWKV, N_DIR * LORA_W, N_DIR * LORA_A, LORA_G, D_MODEL, D_MODEL)
D_IN = sum(IN_WIDTHS)
SPLIT_POINTS = tuple(sum(IN_WIDTHS[:i + 1]) for i in range(len(IN_WIDTHS) - 1))

CHUNK = 64
CHUNKS_PER_STEP = 4
LANES = 128
VMEM_LIMIT = 48 * 1024 * 1024

_HI = lax.Precision.HIGHEST
_F32 = jnp.float32
_BF16 = jnp.bfloat16


def _dot(a, b, precise):
    if precise:
        return jnp.dot(a, b, precision=_HI, preferred_element_type=_F32)
    return jnp.dot(a.astype(_BF16), b.astype(_BF16), preferred_element_type=_F32)


def _mm_kernel(a_ref, b_ref, o_ref, *, precise):
    o_ref[...] = _dot(a_ref[...], b_ref[...], precise)


def matmul(a, b, *, precise=False, tm=512, tn=512):
    m, k = a.shape
    n = b.shape[1]
    tm = min(tm, m)
    tn = min(tn, n)
    return pl.pallas_call(
        functools.partial(_mm_kernel, precise=precise),
        grid=(pl.cdiv(m, tm), pl.cdiv(n, tn)),
        in_specs=[pl.BlockSpec((tm, k), lambda i, j: (i, 0)),
                  pl.BlockSpec((k, tn), lambda i, j: (0, j))],
        out_specs=pl.BlockSpec((tm, tn), lambda i, j: (i, j)),
        out_shape=jax.ShapeDtypeStruct((m, n), _F32),
        compiler_params=pltpu.CompilerParams(
            dimension_semantics=("parallel", "parallel"), vmem_limit_bytes=VMEM_LIMIT),
    )(a, b)


def _inproj_kernel(x_ref, sc_ref, sh_ref, w_ref, o_ref):
    h = x_ref[0] * (1.0 + sc_ref[0]) + sh_ref[0]
    o_ref[0] = _dot(h, w_ref[...], False)


def in_projection(x, sc, sh, w, *, tm=512, tn=512):
    b, t, d = x.shape
    n = w.shape[1]
    tm = min(tm, t)
    return pl.pallas_call(
        _inproj_kernel,
        grid=(b, t // tm, pl.cdiv(n, tn)),
        in_specs=[pl.BlockSpec((1, tm, d), lambda i, j, k: (i, j, 0)),
                  pl.BlockSpec((1, 1, d), lambda i, j, k: (i, 0, 0)),
                  pl.BlockSpec((1, 1, d), lambda i, j, k: (i, 0, 0)),
                  pl.BlockSpec((d, tn), lambda i, j, k: (0, k))],
        out_specs=pl.BlockSpec((1, tm, tn), lambda i, j, k: (i, j, k)),
        out_shape=jax.ShapeDtypeStruct((b, t, n), _F32),
        compiler_params=pltpu.CompilerParams(
            dimension_semantics=("parallel", "parallel", "parallel"), vmem_limit_bytes=VMEM_LIMIT),
    )(x, sc, sh, w)


def _lmm_kernel(w_ref, a_ref, o_ref):
    o_ref[0] = _dot(w_ref[...], a_ref[0], True)


def left_matmul(w, a, *, tn=2048):
    mw, k = w.shape
    g, _, n = a.shape
    tn = min(tn, n)
    return pl.pallas_call(
        _lmm_kernel,
        grid=(g, n // tn),
        in_specs=[pl.BlockSpec((mw, k), lambda i, j: (0, 0)),
                  pl.BlockSpec((1, k, tn), lambda i, j: (i, 0, j))],
        out_specs=pl.BlockSpec((1, mw, tn), lambda i, j: (i, 0, j)),
        out_shape=jax.ShapeDtypeStruct((g, mw, n), _F32),
        compiler_params=pltpu.CompilerParams(
            dimension_semantics=("parallel", "parallel"), vmem_limit_bytes=VMEM_LIMIT),
    )(w, a)


def _dft_cos_sin(n):
    idx = np.arange(n)
    ang = 2.0 * np.pi * ((idx[:, None] * idx[None, :]) % n) / n
    return np.cos(ang), np.sin(ang)


def fourier_mix(u):
    b, t, _ = u.shape
    gd = FOURIER_GROUP_DIM
    cc, sc = _dft_cos_sin(gd)
    w_c = jnp.asarray(np.concatenate([cc, sc], axis=1) / np.sqrt(gd), _F32)
    pq = matmul(u.reshape(b * t * N_FOURIER_GROUPS, gd), w_c, precise=True, tm=2048, tn=2 * gd)
    pq = pq.reshape(b, t, N_FOURIER_GROUPS, 2, gd)
    if t <= 512:
        ct, st = _dft_cos_sin(t)
        w_t = jnp.asarray(np.concatenate([ct, st], axis=0) / np.sqrt(t), _F32)
        m = left_matmul(w_t, pq.reshape(b, t, -1))
        m = m.reshape(b, 2, t, N_FOURIER_GROUPS, 2, gd)
        out = m[:, 0, :, :, 0] - m[:, 1, :, :, 1]
        return out.reshape(b, t, D_FOURIER)
    n1 = int(round(np.sqrt(t)))
    assert n1 * n1 == t
    c1, s1 = _dft_cos_sin(n1)
    w1 = jnp.asarray(np.concatenate([c1, s1], axis=0) / np.sqrt(n1), _F32)
    m1 = left_matmul(w1, pq.reshape(b, n1, -1))
    m1 = m1.reshape(b, 2, n1, n1, N_FOURIER_GROUPS, 2, gd)
    x1r = m1[:, 0, :, :, :, 0] - m1[:, 1, :, :, :, 1]
    x1i = -(m1[:, 0, :, :, :, 1] + m1[:, 1, :, :, :, 0])
    tw = 2.0 * np.pi * (np.arange(n1)[:, None] * np.arange(n1)[None, :]) / t
    twc = jnp.asarray(np.cos(tw), _F32)[None, :, :, None, None]
    tws = jnp.asarray(np.sin(tw), _F32)[None, :, :, None, None]
    ar = x1r * twc + x1i * tws
    ai = x1i * twc - x1r * tws
    a2 = jnp.stack([ar, ai], axis=2).reshape(b * n1, 2 * n1, D_FOURIER)
    w2 = jnp.asarray(np.concatenate([c1, s1], axis=1) / np.sqrt(n1), _F32)
    o = left_matmul(w2, a2)
    o = o.reshape(b, n1, n1, D_FOURIER).transpose(0, 2, 1, 3)
    return o.reshape(b, t, D_FOURIER)


def _scan_masks(reverse):
    L = CHUNK
    t = np.arange(L)[:, None]
    s = np.arange(L)[None, :]
    if reverse:
        t, s = s, t
    strict = (s < t)
    incl = (s <= t)
    levels = []
    bs = 1
    while bs < L:
        levels.append((t // (2 * bs) == s // (2 * bs)) & (t % (2 * bs) >= bs) & (s % (2 * bs) < bs))
        bs *= 2
    eye = np.eye(L, dtype=bool)
    m = np.stack([incl, strict, eye] + levels).astype(np.float32)
    return m


N_LEVELS = int(np.log2(CHUNK))


def _hdot(a, b):
    return jnp.dot(a, b, precision=_HI, preferred_element_type=_F32)


def _hdot_nt(a, b):
    return lax.dot_general(a, b, (((1,), (1,)), ((), ())), precision=_HI, preferred_element_type=_F32)


def _hdot_tn(a, b):
    return lax.dot_general(a, b, (((0,), (0,)), ((), ())), precision=_HI, preferred_element_type=_F32)


def _wkv_kernel(m_ref, r_ref, kk_ref, v_ref, lw_ref, kd_ref, b_ref, s0_ref, y_ref, st_ref, s_scr,
                *, nsub, reverse):
    L, N = CHUNK, HEAD_DIM
    j = pl.program_id(2)

    @pl.when(j == 0)
    def _():
        s_scr[...] = s0_ref[0]

    incl = m_ref[0]
    strict = m_ref[1]
    eye = m_ref[2]
    order = range(nsub - 1, -1, -1) if reverse else range(nsub)
    last_row = 0 if reverse else L - 1

    prepared = {}
    for i in order:
        rows = slice(i * L, (i + 1) * L)
        lw = lw_ref[0, rows, :]
        cum = _hdot(incl, lw)
        tot = cum[last_row:last_row + 1, :]
        g_in = jnp.exp(cum)
        g_ex = jnp.exp(cum - lw)
        g_iv = jnp.exp(-cum)
        g_rem = jnp.exp(tot - cum)
        r = r_ref[0, rows, :]
        kk = kk_ref[0, rows, :]
        v = v_ref[0, rows, :]
        kd = kd_ref[0, rows, :]
        bb = b_ref[0, rows, :]
        rt = r * g_in
        at = -kk * g_ex
        bt = bb * g_iv
        kt = kd * g_iv
        bh = bb * g_rem
        kh = kd * g_rem
        gl = jnp.exp(tot)
        for hh in range(2):
            cs = slice(hh * N, (hh + 1) * N)
            at_h, rt_h, bt_h, kt_h, v_h = at[:, cs], rt[:, cs], bt[:, cs], kt[:, cs], v[:, cs]
            a_ab = _hdot_nt(at_h, bt_h) * strict
            a_ak = _hdot_nt(at_h, kt_h) * strict
            a_rb = _hdot_nt(rt_h, bt_h) * incl
            a_rk = _hdot_nt(rt_h, kt_h) * incl
            tinv = eye + a_ab * m_ref[3]
            for lv in range(1, N_LEVELS):
                tinv = tinv + _hdot(_hdot(tinv, a_ab * m_ref[3 + lv]), tinv)
            wt = _hdot(tinv, at_h)
            u0 = _hdot(tinv, _hdot(a_ak, v_h))
            yv = _hdot(a_rk, v_h)
            prepared[(i, hh)] = (wt, u0, yv, rt_h, a_rb, v_h, bh[:, cs], kh[:, cs], gl[:, cs])

    for i in order:
        rows = slice(i * L, (i + 1) * L)
        ys = []
        for hh in range(2):
            wt, u0, yv, rt_h, a_rb, v_h, bh_h, kh_h, gl_h = prepared[(i, hh)]
            s0 = s_scr[hh]
            u = _hdot_nt(wt, s0) + u0
            ys.append(_hdot_nt(rt_h, s0) + _hdot(a_rb, u) + yv)
            s_scr[hh] = s0 * gl_h + _hdot_tn(u, bh_h) + _hdot_tn(v_h, kh_h)
        y_ref[0, rows, :] = jnp.concatenate(ys, axis=1)

    @pl.when(j == pl.num_programs(2) - 1)
    def _():
        st_ref[0] = s_scr[...]


def wkv_scan(r, kk, v, lw, kd, bb, s0, reverse):
    b, t, d = r.shape
    nsub = CHUNKS_PER_STEP
    tb = nsub * CHUNK
    nblk = t // tb
    npair = d // LANES
    masks = jnp.asarray(_scan_masks(reverse))
    if reverse:
        seq_map = lambda i, p, j: (i, nblk - 1 - j, p)
    else:
        seq_map = lambda i, p, j: (i, j, p)
    seq_spec = pl.BlockSpec((1, tb, LANES), seq_map)
    st_spec = pl.BlockSpec((1, 2, HEAD_DIM, HEAD_DIM), lambda i, p, j: (i, p, 0, 0))
    y, st = pl.pallas_call(
        functools.partial(_wkv_kernel, nsub=nsub, reverse=reverse),
        grid=(b, npair, nblk),
        in_specs=[pl.BlockSpec(masks.shape, lambda i, p, j: (0, 0, 0))] + [seq_spec] * 6 + [st_spec],
        out_specs=[seq_spec, st_spec],
        out_shape=[jax.ShapeDtypeStruct((b, t, d), _F32),
                   jax.ShapeDtypeStruct(s0.shape, _F32)],
        scratch_shapes=[pltpu.VMEM((2, HEAD_DIM, HEAD_DIM), _F32)],
        compiler_params=pltpu.CompilerParams(
            dimension_semantics=("parallel", "parallel", "arbitrary"), vmem_limit_bytes=VMEM_LIMIT),
    )(masks, r, kk, v, lw, kd, bb, s0)
    return y, st


def _expert_kernel(x_ref, g_ref, w1_ref, w3_ref, w2_ref, o_ref):
    f = pl.program_id(2)
    x = x_ref[0].astype(_BF16)
    h1 = jnp.dot(x, w1_ref[0].astype(_BF16), preferred_element_type=_F32)
    h3 = jnp.dot(x, w3_ref[0].astype(_BF16), preferred_element_type=_F32)
    he = (h1 * jax.nn.sigmoid(h1)) * h3
    part = jnp.dot(he.astype(_BF16), w2_ref[0].astype(_BF16), preferred_element_type=_F32)

    @pl.when(f == 0)
    def _():
        o_ref[0] = part

    @pl.when(f > 0)
    def _():
        o_ref[0] += part

    @pl.when(f == pl.num_programs(2) - 1)
    def _():
        o_ref[0] = o_ref[0] * g_ref[0]


def expert_ffn(xe, gate, w1, w3, w2, *, tm=1024, tf=512):
    e, c, d = xe.shape
    f = w1.shape[2]
    tm = min(tm, c)
    return pl.pallas_call(
        _expert_kernel,
        grid=(e, c // tm, f // tf),
        in_specs=[pl.BlockSpec((1, tm, d), lambda i, j, k: (i, j, 0)),
                  pl.BlockSpec((1, tm, 1), lambda i, j, k: (i, j, 0)),
                  pl.BlockSpec((1, d, tf), lambda i, j, k: (i, 0, k)),
                  pl.BlockSpec((1, d, tf), lambda i, j, k: (i, 0, k)),
                  pl.BlockSpec((1, tf, d), lambda i, j, k: (i, k, 0))],
        out_specs=pl.BlockSpec((1, tm, d), lambda i, j, k: (i, j, 0)),
        out_shape=jax.ShapeDtypeStruct((e, c, d), _F32),
        compiler_params=pltpu.CompilerParams(
            dimension_semantics=("parallel", "parallel", "arbitrary"), vmem_limit_bytes=VMEM_LIMIT),
    )(xe, gate, w1, w3, w2)


def _layer_norm(x, g, b):
    mu = jnp.mean(x, axis=-1, keepdims=True)
    var = jnp.mean(jnp.square(x - mu), axis=-1, keepdims=True)
    return (x - mu) * lax.rsqrt(var + LN_EPS) * g + b


def _centred_conv(u, w):
    t = u.shape[1]
    pad = SHORT_CONV // 2
    up = jnp.pad(u, ((0, 0), (pad, pad), (0, 0)))
    out = up[:, 0:t] * w[0]
    for i in range(1, SHORT_CONV):
        out = out + up[:, i:i + t] * w[i]
    return out


def _mixer(x, sc1, sh1, s_f0, s_b0, p):
    b, t, d = x.shape
    m = b * t
    proj = in_projection(x, sc1, sh1, p["w_in"])
    u_f, rkv, w_dn, a_dn, g_dn, gate_f, gate_r = jnp.split(proj, SPLIT_POINTS, axis=-1)
    o_f = matmul(fourier_mix(u_f).reshape(m, D_FOURIER), p["w_fo"]).reshape(b, t, d)
    rkv = _centred_conv(rkv, p["conv_w"])
    r, k, v = jnp.split(rkv, 3, axis=-1)
    kk = (k * p["k_k"]).reshape(b, t, N_HEADS, HEAD_DIM)
    kk = kk * lax.rsqrt(jnp.sum(jnp.square(kk), axis=-1, keepdims=True) + 1e-12)
    kk = kk.reshape(b, t, D_RWKV)
    g = matmul(jax.nn.sigmoid(g_dn).reshape(m, LORA_G), p["g_up"], precise=True).reshape(b, t, D_RWKV)
    w_dn = w_dn.reshape(m, N_DIR, LORA_W)
    a_dn = a_dn.reshape(m, N_DIR, LORA_A)
    ys, bonus, states = [], [], []
    for di, (s0, reverse) in enumerate(((s_f0, False), (s_b0, True))):
        w_up = matmul(jnp.tanh(w_dn[:, di]), p["w_decay_up"][di], precise=True).reshape(b, t, D_RWKV)
        a_up = matmul(a_dn[:, di], p["a_up"][di], precise=True).reshape(b, t, D_RWKV)
        w_log = -jax.nn.softplus(-(p["w_decay0"][di] + w_up)) - 0.5
        lw = -jnp.exp(w_log)
        a_rate = jax.nn.sigmoid(p["a0"][di] + a_up)
        k_d = k * (1.0 + (a_rate - 1.0) * p["k_a"])
        y, st = wkv_scan(r, kk, v, lw, k_d, kk * a_rate, s0, reverse)
        ys.append(y)
        states.append(st)
        coef = jnp.sum((r * k_d * p["r_k"].reshape(-1)).reshape(b, t, N_HEADS, HEAD_DIM), axis=-1, keepdims=True)
        bonus.append(coef * v.reshape(b, t, N_HEADS, HEAD_DIM))
    y = (ys[0] + ys[1]).reshape(b, t, N_HEADS, HEAD_DIM)
    mu = jnp.mean(y, axis=-1, keepdims=True)
    var = jnp.mean(jnp.square(y - mu), axis=-1, keepdims=True)
    y = ((y - mu) * lax.rsqrt(var + GN_EPS)).reshape(b, t, D_RWKV) * p["gn_g"] + p["gn_b"]
    y = y + (bonus[0] + bonus[1]).reshape(b, t, D_RWKV)
    o_r = matmul((y * g).reshape(m, D_RWKV), p["w_ro"]).reshape(b, t, d)
    merged = jax.nn.sigmoid(gate_f) * o_f + jax.nn.sigmoid(gate_r) * o_r
    mix = matmul(merged.reshape(m, d), p["w_out"]).reshape(b, t, d)
    return mix, states[0], states[1]


def _expert_choice_ffn(h, p):
    b, t, d = h.shape
    n = b * t
    cap = CAPACITY_FACTOR * n // N_EXPERTS
    hf = h.reshape(n, d)
    logits = matmul(hf, p["w_router"], precise=True, tm=1024)
    aff = jax.nn.softmax(logits, axis=-1)
    gate, idx = lax.top_k(aff.T, cap)
    xe = hf[idx]
    ye = expert_ffn(xe, gate[..., None], p["w_e1"], p["w_e3"], p["w_e2"])
    out = jnp.zeros_like(hf).at[idx.reshape(-1)].add(ye.reshape(-1, d))
    return out.reshape(b, t, d)


def _trunk_layer(x, cond, s_f0, s_b0, p):
    mod = matmul(jax.nn.silu(cond), p["w_ada"], precise=True) + p["b_ada"]
    sh1, sc1, g1, sh2, sc2, g2 = [a[:, None, :] for a in jnp.split(mod, 6, axis=-1)]
    mix, s_f, s_b = _mixer(x, sc1, sh1, s_f0, s_b0, p)
    x = _layer_norm(DEEPNORM_ALPHA * x + g1 * mix, p["ln1_g"], p["ln1_b"])
    h2 = x * (1.0 + sc2) + sh2
    x = _layer_norm(DEEPNORM_ALPHA * x + g2 * _expert_choice_ffn(h2, p), p["ln2_g"], p["ln2_b"])
    return x, s_f, s_b


def kernel(x_prompt, x_sample, state_fwd, state_bwd, c, c_ctx, w_ada, b_ada, w_in, conv_w, w_decay0, w_decay_up, a0, a_up, g_up, k_k, k_a, r_k, gn_g, gn_b, w_fo, w_ro, w_out, ln1_g, ln1_b, w_router, w_e1, w_e3, w_e2, ln2_g, ln2_b):
    n_ctx = x_prompt.shape[0]
    cond_ctx = jnp.broadcast_to(c_ctx, (n_ctx, D_MODEL))
    zero_state = jnp.zeros((n_ctx, N_HEADS, HEAD_DIM, HEAD_DIM), _F32)
    names = ("w_ada", "b_ada", "w_in", "conv_w", "w_decay0", "w_decay_up", "a0", "a_up", "g_up", "k_k", "k_a",
             "r_k", "gn_g", "gn_b", "w_fo", "w_ro", "w_out", "ln1_g", "ln1_b", "w_router", "w_e1", "w_e3",
             "w_e2", "ln2_g", "ln2_b")
    stacked = (w_ada, b_ada, w_in, conv_w, w_decay0, w_decay_up, a0, a_up, g_up, k_k, k_a, r_k, gn_g, gn_b,
               w_fo, w_ro, w_out, ln1_g, ln1_b, w_router, w_e1, w_e3, w_e2, ln2_g, ln2_b)
    xp, xs = x_prompt, x_sample
    new_f, new_b = [], []
    for l in range(DEPTH):
        p = {k: a[l] for k, a in zip(names, stacked)}
        xp, s_f, s_b = _trunk_layer(xp, cond_ctx, zero_state, zero_state, p)
        new_f.append(s_f)
        new_b.append(s_b)
        xs, _, _ = _trunk_layer(xs, c, state_fwd[:, l], state_bwd[:, l], p)
    return (xp, xs, jnp.stack(new_f, axis=1), jnp.stack(new_b, axis=1))
```

```python
import functools

import numpy as np
import jax
import jax.numpy as jnp
from jax import lax
from jax.experimental import pallas as pl
from jax.experimental.pallas import tpu as pltpu

D_MODEL = 1024
N_FOURIER_GROUPS = 4
FOURIER_GROUP_DIM = 128
D_FOURIER = N_FOURIER_GROUPS * FOURIER_GROUP_DIM
HEAD_DIM = 64
D_RWKV = D_MODEL
N_HEADS = D_RWKV // HEAD_DIM
LORA_W = 64
LORA_A = 64
LORA_G = 128
N_DIR = 2
SHORT_CONV = 3
GN_EPS = 64e-5
N_EXPERTS = 16
CAPACITY_FACTOR = 2
D_EXPERT = 2048
LN_EPS = 1e-5
DEPTH = 1
DEEPNORM_ALPHA = (2.0 * DEPTH) ** 0.25
IN_WIDTHS = (D_FOURIER, 3 * D_RWKV, N_DIR * LORA_W, N_DIR * LORA_A, LORA_G, D_MODEL, D_MODEL)
D_IN = sum(IN_WIDTHS)
SPLIT_POINTS = tuple(sum(IN_WIDTHS[:i + 1]) for i in range(len(IN_WIDTHS) - 1))

LANES = 128
CHUNK = HEAD_DIM
CHUNKS_PER_STEP = 4
PAIRS_PER_STEP = 4
VMEM_LIMIT = 48 * 1024 * 1024

_HI = lax.Precision.HIGHEST
_F32 = jnp.float32
_BF16 = jnp.bfloat16


def _dot(a, b, precise):
    if precise:
        return jnp.dot(a, b, precision=_HI, preferred_element_type=_F32)
    return jnp.dot(a.astype(_BF16), b.astype(_BF16), preferred_element_type=_F32)


def _mm_kernel(a_ref, b_ref, o_ref, *, precise):
    o_ref[...] = _dot(a_ref[...], b_ref[...], precise)


def matmul(a, b, *, precise=False, tm=512, tn=512):
    m, k = a.shape
    n = b.shape[1]
    tm = min(tm, m)
    tn = min(tn, n)
    return pl.pallas_call(
        functools.partial(_mm_kernel, precise=precise),
        grid=(pl.cdiv(m, tm), pl.cdiv(n, tn)),
        in_specs=[pl.BlockSpec((tm, k), lambda i, j: (i, 0)),
                  pl.BlockSpec((k, tn), lambda i, j: (0, j))],
        out_specs=pl.BlockSpec((tm, tn), lambda i, j: (i, j)),
        out_shape=jax.ShapeDtypeStruct((m, n), _F32),
        compiler_params=pltpu.CompilerParams(
            dimension_semantics=("parallel", "parallel"), vmem_limit_bytes=VMEM_LIMIT),
    )(a, b)


def _inproj_kernel(x_ref, sc_ref, sh_ref, w_ref, o_ref):
    h = x_ref[0] * (1.0 + sc_ref[0]) + sh_ref[0]
    o_ref[0] = _dot(h, w_ref[...], False)


def in_projection(x, sc, sh, w, *, tm=512, tn=512):
    b, t, d = x.shape
    n = w.shape[1]
    tm = min(tm, t)
    return pl.pallas_call(
        _inproj_kernel,
        grid=(b, t // tm, pl.cdiv(n, tn)),
        in_specs=[pl.BlockSpec((1, tm, d), lambda i, j, k: (i, j, 0)),
                  pl.BlockSpec((1, 1, d), lambda i, j, k: (i, 0, 0)),
                  pl.BlockSpec((1, 1, d), lambda i, j, k: (i, 0, 0)),
                  pl.BlockSpec((d, tn), lambda i, j, k: (0, k))],
        out_specs=pl.BlockSpec((1, tm, tn), lambda i, j, k: (i, j, k)),
        out_shape=jax.ShapeDtypeStruct((b, t, n), _F32),
        compiler_params=pltpu.CompilerParams(
            dimension_semantics=("parallel", "parallel", "parallel"), vmem_limit_bytes=VMEM_LIMIT),
    )(x, sc, sh, w)


def _lmm_kernel(w_ref, a_ref, o_ref):
    o_ref[0] = _dot(w_ref[...], a_ref[0], True)


def left_matmul(w, a, *, tn=2048):
    mw, k = w.shape
    g, _, n = a.shape
    tn = min(tn, n)
    return pl.pallas_call(
        _lmm_kernel,
        grid=(g, n // tn),
        in_specs=[pl.BlockSpec((mw, k), lambda i, j: (0, 0)),
                  pl.BlockSpec((1, k, tn), lambda i, j: (i, 0, j))],
        out_specs=pl.BlockSpec((1, mw, tn), lambda i, j: (i, 0, j)),
        out_shape=jax.ShapeDtypeStruct((g, mw, n), _F32),
        compiler_params=pltpu.CompilerParams(
            dimension_semantics=("parallel", "parallel"), vmem_limit_bytes=VMEM_LIMIT),
    )(w, a)


def _dft_cos_sin(n):
    idx = np.arange(n)
    ang = 2.0 * np.pi * ((idx[:, None] * idx[None, :]) % n) / n
    return np.cos(ang), np.sin(ang)


def fourier_mix(u):
    b, t, _ = u.shape
    gd = FOURIER_GROUP_DIM
    cc, sc = _dft_cos_sin(gd)
    w_c = jnp.asarray(np.concatenate([cc, sc], axis=1) / np.sqrt(gd), _F32)
    pq = matmul(u.reshape(b * t * N_FOURIER_GROUPS, gd), w_c, precise=True, tm=2048, tn=2 * gd)
    pq = pq.reshape(b, t, N_FOURIER_GROUPS, 2, gd)
    if t <= 512:
        ct, st = _dft_cos_sin(t)
        w_t = jnp.asarray(np.concatenate([ct, st], axis=0) / np.sqrt(t), _F32)
        m = left_matmul(w_t, pq.reshape(b, t, -1))
        m = m.reshape(b, 2, t, N_FOURIER_GROUPS, 2, gd)
        out = m[:, 0, :, :, 0] - m[:, 1, :, :, 1]
        return out.reshape(b, t, D_FOURIER)
    n1 = int(round(np.sqrt(t)))
    assert n1 * n1 == t
    c1, s1 = _dft_cos_sin(n1)
    w1 = jnp.asarray(np.concatenate([c1, s1], axis=0) / np.sqrt(n1), _F32)
    m1 = left_matmul(w1, pq.reshape(b, n1, -1))
    m1 = m1.reshape(b, 2, n1, n1, N_FOURIER_GROUPS, 2, gd)
    x1r = m1[:, 0, :, :, :, 0] - m1[:, 1, :, :, :, 1]
    x1i = -(m1[:, 0, :, :, :, 1] + m1[:, 1, :, :, :, 0])
    tw = 2.0 * np.pi * (np.arange(n1)[:, None] * np.arange(n1)[None, :]) / t
    twc = jnp.asarray(np.cos(tw), _F32)[None, :, :, None, None]
    tws = jnp.asarray(np.sin(tw), _F32)[None, :, :, None, None]
    ar = x1r * twc + x1i * tws
    ai = x1i * twc - x1r * tws
    a2 = jnp.stack([ar, ai], axis=2).reshape(b * n1, 2 * n1, D_FOURIER)
    w2 = jnp.asarray(np.concatenate([c1, s1], axis=1) / np.sqrt(n1), _F32)
    o = left_matmul(w2, a2)
    o = o.reshape(b, n1, n1, D_FOURIER).transpose(0, 2, 1, 3)
    return o.reshape(b, t, D_FOURIER)


N_LEVELS = int(np.log2(CHUNK))
SCAN_PASSES = {"gram": 1, "tinv": 1, "apply": 1, "state": 1}

_NN = (((1,), (0,)), ((), ()))
_NT = (((1,), (1,)), ((), ()))
_TN = (((0,), (0,)), ((), ()))

_C_INCL, _C_STRICT, _C_EYE, _C_LEVEL0 = 0, 1, 2, 3
_C_LEFT = _C_LEVEL0 + N_LEVELS
_C_RIGHT = _C_LEFT + 1


def _scan_consts(reverse):
    L = CHUNK
    t = np.arange(L)[:, None]
    s = np.arange(L)[None, :]
    if reverse:
        t, s = s, t
    strict = (s < t)
    incl = (s <= t)
    levels = []
    bs = 1
    while bs < L:
        levels.append((t // (2 * bs) == s // (2 * bs)) & (t % (2 * bs) >= bs) & (s % (2 * bs) < bs))
        bs *= 2
    pair = lambda m: np.concatenate([m, m], axis=1)
    left = np.concatenate([np.ones((L, HEAD_DIM)), np.zeros((L, HEAD_DIM))], axis=1)
    slabs = [pair(incl), pair(strict), pair(np.eye(L))] + [pair(l) for l in levels] + [left, 1.0 - left]
    return np.stack(slabs).astype(np.float32)


def _pdot(a, b, dims, passes):
    dg = lambda x, y: lax.dot_general(x, y, dims, preferred_element_type=_F32)
    ah = a.astype(_BF16)
    bh = b.astype(_BF16)
    if passes == 1:
        return dg(ah, bh)
    al = (a - ah.astype(_F32)).astype(_BF16)
    bl = (b - bh.astype(_F32)).astype(_BF16)
    return dg(ah, bh) + dg(ah, bl) + dg(al, bh)


def _wkv_kernel(c_ref, r_ref, kk_ref, v_ref, lw_ref, kd_ref, b_ref, s0_ref, y_ref, st_ref, s_scr,
                *, nsub, npp, reverse):
    L, N = CHUNK, HEAD_DIM
    j = pl.program_id(2)
    p_gram, p_tinv, p_apply, p_state = (SCAN_PASSES[k] for k in ("gram", "tinv", "apply", "state"))
    inclp = c_ref[_C_INCL]
    strictp = c_ref[_C_STRICT]
    left = c_ref[_C_LEFT]
    right = c_ref[_C_RIGHT]
    bdmask = jnp.concatenate([left, right], axis=0)
    incl_b = inclp[:, :L].astype(_BF16)

    def bd(x):
        return jnp.concatenate([x * left, x * right], axis=0)

    @pl.when(j == 0)
    def _():
        zero = jnp.zeros((N, N), _F32)
        for q in range(npp):
            top = jnp.concatenate([s0_ref[0, 2 * q], zero], axis=1)
            bot = jnp.concatenate([zero, s0_ref[0, 2 * q + 1]], axis=1)
            s_scr[q] = jnp.concatenate([top, bot], axis=0)

    order = range(nsub - 1, -1, -1) if reverse else range(nsub)
    last_row = 0 if reverse else L - 1

    chains = [(q, i) for i in order for q in range(npp)]

    def each(fn, *lists):
        return [fn(*args) for args in zip(*lists)]

    def tile(ref, c):
        q, i = c
        return ref[0, i * L:(i + 1) * L, q * LANES:(q + 1) * LANES]

    lw = [tile(lw_ref, c) for c in chains]

    def cumulative(x):
        x_h = x.astype(_BF16)
        rem = x - x_h.astype(_F32)
        x_m = rem.astype(_BF16)
        x_l = (rem - x_m.astype(_F32)).astype(_BF16)
        return (jnp.dot(incl_b, x_h, preferred_element_type=_F32)
                + jnp.dot(incl_b, x_m, preferred_element_type=_F32)
                + jnp.dot(incl_b, x_l, preferred_element_type=_F32))

    cum = each(cumulative, lw)
    tot = [x[last_row:last_row + 1, :] for x in cum]
    g_iv = [jnp.exp(-x) for x in cum]
    g_rem = each(lambda t_, x: jnp.exp(t_ - x), tot, cum)
    v = [tile(v_ref, c) for c in chains]
    rt = [tile(r_ref, c) * jnp.exp(x) for c, x in zip(chains, cum)]
    at = [-tile(kk_ref, c) * jnp.exp(x - w) for c, x, w in zip(chains, cum, lw)]
    kd = [tile(kd_ref, c) for c in chains]
    bb = [tile(b_ref, c) for c in chains]
    bh = each(lambda x, g: x * g, bb, g_rem)
    kh = each(lambda x, g: x * g, kd, g_rem)
    gram = each(lambda a, r_, b_, k_, g: _pdot(jnp.concatenate([a, r_], axis=0),
                                               jnp.concatenate([bd(b_ * g), bd(k_ * g)], axis=0), _NT, p_gram),
                at, rt, bb, kd, g_iv)
    a_ab = [g[:L, :LANES] * strictp for g in gram]
    a_ak = [g[:L, LANES:] * strictp for g in gram]
    a_rb = [g[L:, :LANES] * inclp for g in gram]
    a_rk = [g[L:, LANES:] * inclp for g in gram]
    tinv = [c_ref[_C_EYE] + a * c_ref[_C_LEVEL0] for a in a_ab]
    for lv in range(1, N_LEVELS):
        tx = each(lambda t_, a: _pdot(t_, bd(a * c_ref[_C_LEVEL0 + lv]), _NN, p_tinv), tinv, a_ab)
        tinv = each(lambda t_, x: t_ + _pdot(x, bd(t_), _NN, p_tinv), tinv, tx)
    wt = each(lambda t_, a: _pdot(t_, bd(a), _NN, p_apply), tinv, at)
    av = each(lambda a, v_: _pdot(a, bd(v_), _NN, p_apply), a_ak, v)
    u0 = each(lambda t_, x: _pdot(t_, bd(x), _NN, p_apply), tinv, av)
    qq = each(lambda r_, a, w: r_ + _pdot(a, bd(w), _NN, p_apply), rt, a_rb, wt)
    y0 = each(lambda ab, ak, u, v_: _pdot(jnp.concatenate([ab, ak], axis=1),
                                          jnp.concatenate([bd(u), bd(v_)], axis=0), _NN, p_apply),
              a_rb, a_rk, u0, v)
    pm = each(lambda w, b_: _pdot(w, b_, _TN, p_state) * bdmask, wt, bh)
    cp = each(lambda u, v_, b_, k_: _pdot(jnp.concatenate([u, v_], axis=0), jnp.concatenate([b_, k_], axis=0),
                                          _TN, p_state) * bdmask, u0, v, bh, kh)
    gl = [jnp.exp(x) for x in tot]

    for n, (q, i) in enumerate(chains):
        s0 = s_scr[q]
        y_ref[0, i * L:(i + 1) * L, q * LANES:(q + 1) * LANES] = _pdot(qq[n], s0, _NT, p_state) + y0[n]
        s_scr[q] = s0 * gl[n] + _pdot(s0, pm[n], _NN, p_state) + cp[n]

    @pl.when(j == pl.num_programs(2) - 1)
    def _():
        for q in range(npp):
            s = s_scr[q]
            st_ref[0, 2 * q] = s[:N, :N]
            st_ref[0, 2 * q + 1] = s[N:, N:]


def wkv_scan(r, kk, v, lw, kd, bb, s0, reverse):
    b, t, d = r.shape
    nsub = CHUNKS_PER_STEP
    npp = PAIRS_PER_STEP
    tb = nsub * CHUNK
    nblk = t // tb
    ngrp = d // (LANES * npp)
    consts = jnp.asarray(_scan_consts(reverse))
    if reverse:
        seq_map = lambda i, p, j: (i, nblk - 1 - j, p)
    else:
        seq_map = lambda i, p, j: (i, j, p)
    seq_spec = pl.BlockSpec((1, tb, LANES * npp), seq_map)
    st_spec = pl.BlockSpec((1, 2 * npp, HEAD_DIM, HEAD_DIM), lambda i, p, j: (i, p, 0, 0))
    y, st = pl.pallas_call(
        functools.partial(_wkv_kernel, nsub=nsub, npp=npp, reverse=reverse),
        grid=(b, ngrp, nblk),
        in_specs=[pl.BlockSpec(consts.shape, lambda i, p, j: (0, 0, 0))] + [seq_spec] * 6 + [st_spec],
        out_specs=[seq_spec, st_spec],
        out_shape=[jax.ShapeDtypeStruct((b, t, d), _F32),
                   jax.ShapeDtypeStruct(s0.shape, _F32)],
        scratch_shapes=[pltpu.VMEM((npp, LANES, LANES), _F32)],
        compiler_params=pltpu.CompilerParams(
            dimension_semantics=("parallel", "parallel", "arbitrary"), vmem_limit_bytes=VMEM_LIMIT),
        name="wkv_scan_bwd" if reverse else "wkv_scan_fwd",
    )(consts, r, kk, v, lw, kd, bb, s0)
    return y, st


def _expert_kernel(x_ref, g_ref, w1_ref, w3_ref, w2_ref, o_ref):
    f = pl.program_id(2)
    x = x_ref[0].astype(_BF16)
    h1 = jnp.dot(x, w1_ref[0].astype(_BF16), preferred_element_type=_F32)
    h3 = jnp.dot(x, w3_ref[0].astype(_BF16), preferred_element_type=_F32)
    he = (h1 * jax.nn.sigmoid(h1)) * h3
    part = jnp.dot(he.astype(_BF16), w2_ref[0].astype(_BF16), preferred_element_type=_F32)

    @pl.when(f == 0)
    def _():
        o_ref[0] = part

    @pl.when(f > 0)
    def _():
        o_ref[0] += part

    @pl.when(f == pl.num_programs(2) - 1)
    def _():
        o_ref[0] = o_ref[0] * g_ref[0]


def expert_ffn(xe, gate, w1, w3, w2, *, tm=1024, tf=512):
    e, c, d = xe.shape
    f = w1.shape[2]
    tm = min(tm, c)
    return pl.pallas_call(
        _expert_kernel,
        grid=(e, c // tm, f // tf),
        in_specs=[pl.BlockSpec((1, tm, d), lambda i, j, k: (i, j, 0)),
                  pl.BlockSpec((1, tm, 1), lambda i, j, k: (i, j, 0)),
                  pl.BlockSpec((1, d, tf), lambda i, j, k: (i, 0, k)),
                  pl.BlockSpec((1, d, tf), lambda i, j, k: (i, 0, k)),
                  pl.BlockSpec((1, tf, d), lambda i, j, k: (i, k, 0))],
        out_specs=pl.BlockSpec((1, tm, d), lambda i, j, k: (i, j, 0)),
        out_shape=jax.ShapeDtypeStruct((e, c, d), _F32),
        compiler_params=pltpu.CompilerParams(
            dimension_semantics=("parallel", "parallel", "arbitrary"), vmem_limit_bytes=VMEM_LIMIT),
        name="expert_ffn",
    )(xe, gate, w1, w3, w2)


def _layer_norm(x, g, b):
    mu = jnp.mean(x, axis=-1, keepdims=True)
    var = jnp.mean(jnp.square(x - mu), axis=-1, keepdims=True)
    return (x - mu) * lax.rsqrt(var + LN_EPS) * g + b


def _centred_conv(u, w):
    t = u.shape[1]
    pad = SHORT_CONV // 2
    up = jnp.pad(u, ((0, 0), (pad, pad), (0, 0)))
    out = up[:, 0:t] * w[0]
    for i in range(1, SHORT_CONV):
        out = out + up[:, i:i + t] * w[i]
    return out


def _mixer(x, sc1, sh1, s_f0, s_b0, p):
    b, t, d = x.shape
    m = b * t
    proj = in_projection(x, sc1, sh1, p["w_in"])
    u_f, rkv, w_dn, a_dn, g_dn, gate_f, gate_r = jnp.split(proj, SPLIT_POINTS, axis=-1)
    o_f = matmul(fourier_mix(u_f).reshape(m, D_FOURIER), p["w_fo"]).reshape(b, t, d)
    rkv = _centred_conv(rkv, p["conv_w"])
    r, k, v = jnp.split(rkv, 3, axis=-1)
    kk = (k * p["k_k"]).reshape(b, t, N_HEADS, HEAD_DIM)
    kk = kk * lax.rsqrt(jnp.sum(jnp.square(kk), axis=-1, keepdims=True) + 1e-12)
    kk = kk.reshape(b, t, D_RWKV)
    g = matmul(jax.nn.sigmoid(g_dn).reshape(m, LORA_G), p["g_up"], precise=True).reshape(b, t, D_RWKV)
    w_dn = w_dn.reshape(m, N_DIR, LORA_W)
    a_dn = a_dn.reshape(m, N_DIR, LORA_A)
    ys, bonus, states = [], [], []
    for di, (s0, reverse) in enumerate(((s_f0, False), (s_b0, True))):
        w_up = matmul(jnp.tanh(w_dn[:, di]), p["w_decay_up"][di], precise=True).reshape(b, t, D_RWKV)
        a_up = matmul(a_dn[:, di], p["a_up"][di], precise=True).reshape(b, t, D_RWKV)
        w_log = -jax.nn.softplus(-(p["w_decay0"][di] + w_up)) - 0.5
        lw = -jnp.exp(w_log)
        a_rate = jax.nn.sigmoid(p["a0"][di] + a_up)
        k_d = k * (1.0 + (a_rate - 1.0) * p["k_a"])
        y, st = wkv_scan(r, kk, v, lw, k_d, kk * a_rate, s0, reverse)
        ys.append(y)
        states.append(st)
        coef = jnp.sum((r * k_d * p["r_k"].reshape(-1)).reshape(b, t, N_HEADS, HEAD_DIM), axis=-1, keepdims=True)
        bonus.append(coef * v.reshape(b, t, N_HEADS, HEAD_DIM))
    y = (ys[0] + ys[1]).reshape(b, t, N_HEADS, HEAD_DIM)
    mu = jnp.mean(y, axis=-1, keepdims=True)
    var = jnp.mean(jnp.square(y - mu), axis=-1, keepdims=True)
    y = ((y - mu) * lax.rsqrt(var + GN_EPS)).reshape(b, t, D_RWKV) * p["gn_g"] + p["gn_b"]
    y = y + (bonus[0] + bonus[1]).reshape(b, t, D_RWKV)
    o_r = matmul((y * g).reshape(m, D_RWKV), p["w_ro"]).reshape(b, t, d)
    merged = jax.nn.sigmoid(gate_f) * o_f + jax.nn.sigmoid(gate_r) * o_r
    mix = matmul(merged.reshape(m, d), p["w_out"]).reshape(b, t, d)
    return mix, states[0], states[1]


def _expert_choice_ffn(h, p):
    b, t, d = h.shape
    n = b * t
    cap = CAPACITY_FACTOR * n // N_EXPERTS
    hf = h.reshape(n, d)
    logits = matmul(hf, p["w_router"], precise=True, tm=1024)
    aff = jax.nn.softmax(logits, axis=-1)
    gate, idx = lax.top_k(aff.T, cap)
    xe = hf[idx]
    ye = expert_ffn(xe, gate[..., None], p["w_e1"], p["w_e3"], p["w_e2"])
    out = jnp.zeros_like(hf).at[idx.reshape(-1)].add(ye.reshape(-1, d))
    return out.reshape(b, t, d)


def _trunk_layer(x, cond, s_f0, s_b0, p):
    mod = matmul(jax.nn.silu(cond), p["w_ada"], precise=True) + p["b_ada"]
    sh1, sc1, g1, sh2, sc2, g2 = [a[:, None, :] for a in jnp.split(mod, 6, axis=-1)]
    mix, s_f, s_b = _mixer(x, sc1, sh1, s_f0, s_b0, p)
    x = _layer_norm(DEEPNORM_ALPHA * x + g1 * mix, p["ln1_g"], p["ln1_b"])
    h2 = x * (1.0 + sc2) + sh2
    x = _layer_norm(DEEPNORM_ALPHA * x + g2 * _expert_choice_ffn(h2, p), p["ln2_g"], p["ln2_b"])
    return x, s_f, s_b


def kernel(x_prompt, x_sample, state_fwd, state_bwd, c, c_ctx, w_ada, b_ada, w_in, conv_w, w_decay0, w_decay_up, a0, a_up, g_up, k_k, k_a, r_k, gn_g, gn_b, w_fo, w_ro, w_out, ln1_g, ln1_b, w_router, w_e1, w_e3, w_e2, ln2_g, ln2_b):
    n_ctx = x_prompt.shape[0]
    cond_ctx = jnp.broadcast_to(c_ctx, (n_ctx, D_MODEL))
    zero_state = jnp.zeros((n_ctx, N_HEADS, HEAD_DIM, HEAD_DIM), _F32)
    names = ("w_ada", "b_ada", "w_in", "conv_w", "w_decay0", "w_decay_up", "a0", "a_up", "g_up", "k_k", "k_a",
             "r_k", "gn_g", "gn_b", "w_fo", "w_ro", "w_out", "ln1_g", "ln1_b", "w_router", "w_e1", "w_e3",
             "w_e2", "ln2_g", "ln2_b")
    stacked = (w_ada, b_ada, w_in, conv_w, w_decay0, w_decay_up, a0, a_up, g_up, k_k, k_a, r_k, gn_g, gn_b,
               w_fo, w_ro, w_out, ln1_g, ln1_b, w_router, w_e1, w_e3, w_e2, ln2_g, ln2_b)
    xp, xs = x_prompt, x_sample
    new_f, new_b = [], []
    for l in range(DEPTH):
        p = {k: a[l] for k, a in zip(names, stacked)}
        xp, s_f, s_b = _trunk_layer(xp, cond_ctx, zero_state, zero_state, p)
        new_f.append(s_f)
        new_b.append(s_b)
        xs, _, _ = _trunk_layer(xs, c, state_fwd[:, l], state_bwd[:, l], p)
    return (xp, xs, jnp.stack(new_f, axis=1), jnp.stack(new_b, axis=1))
```

```python
import functools

import numpy as np
import jax
import jax.numpy as jnp
from jax import lax
from jax.experimental import pallas as pl
from jax.experimental.pallas import tpu as pltpu

D_MODEL = 1024
N_FOURIER_GROUPS = 4
FOURIER_GROUP_DIM = 128
D_FOURIER = N_FOURIER_GROUPS * FOURIER_GROUP_DIM
HEAD_DIM = 64
D_RWKV = D_MODEL
N_HEADS = D_RWKV // HEAD_DIM
LORA_W = 64
LORA_A = 64
LORA_G = 128
N_DIR = 2
SHORT_CONV = 3
GN_EPS = 64e-5
N_EXPERTS = 16
CAPACITY_FACTOR = 2
D_EXPERT = 2048
LN_EPS = 1e-5
DEPTH = 1
DEEPNORM_ALPHA = (2.0 * DEPTH) ** 0.25
IN_WIDTHS = (D_FOURIER, 3 * D_RWKV, N_DIR * LORA_W, N_DIR * LORA_A, LORA_G, D_MODEL, D_MODEL)
D_IN = sum(IN_WIDTHS)
SPLIT_POINTS = tuple(sum(IN_WIDTHS[:i + 1]) for i in range(len(IN_WIDTHS) - 1))

LANES = 128
CHUNK = HEAD_DIM
CHUNKS_PER_STEP = 4
PAIRS_PER_STEP = 4
VMEM_LIMIT = 48 * 1024 * 1024

_HI = lax.Precision.HIGHEST
_F32 = jnp.float32
_BF16 = jnp.bfloat16


def _dot(a, b, precise):
    if precise:
        return jnp.dot(a, b, precision=_HI, preferred_element_type=_F32)
    return jnp.dot(a.astype(_BF16), b.astype(_BF16), preferred_element_type=_F32)


def _mm_kernel(a_ref, b_ref, o_ref, *, precise):
    o_ref[...] = _dot(a_ref[...], b_ref[...], precise)


def matmul(a, b, *, precise=False, tm=512, tn=512):
    m, k = a.shape
    n = b.shape[1]
    tm = min(tm, m)
    tn = min(tn, n)
    return pl.pallas_call(
        functools.partial(_mm_kernel, precise=precise),
        grid=(pl.cdiv(m, tm), pl.cdiv(n, tn)),
        in_specs=[pl.BlockSpec((tm, k), lambda i, j: (i, 0)),
                  pl.BlockSpec((k, tn), lambda i, j: (0, j))],
        out_specs=pl.BlockSpec((tm, tn), lambda i, j: (i, j)),
        out_shape=jax.ShapeDtypeStruct((m, n), _F32),
        compiler_params=pltpu.CompilerParams(
            dimension_semantics=("parallel", "parallel"), vmem_limit_bytes=VMEM_LIMIT),
    )(a, b)


P_R, P_K, P_V = 0, D_RWKV, 2 * D_RWKV
P_GATE_F = 3 * D_RWKV
P_GATE_R = P_GATE_F + D_MODEL
P_UF = P_GATE_R + D_MODEL
P_WDN = P_UF + D_FOURIER
P_ADN = P_WDN + N_DIR * LORA_W
P_GDN = P_ADN + N_DIR * LORA_A
D_PROJ = P_GDN + LORA_G + 128
assert D_PROJ % 512 == 0 and N_DIR * LORA_W == LANES and N_DIR * LORA_A == LANES and LORA_G == LANES


def _permute_in_weight(w_in):
    s = (0,) + SPLIT_POINTS + (D_IN,)
    seg = [w_in[:, s[i]:s[i + 1]] for i in range(len(IN_WIDTHS))]
    pad = jnp.zeros((w_in.shape[0], D_PROJ - D_IN), w_in.dtype)
    return jnp.concatenate([seg[1], seg[5], seg[6], seg[0], seg[2], seg[3], seg[4], pad], axis=1)


def _inproj_kernel(x_ref, sc_ref, sh_ref, w_ref, o_ref, h_scr):
    @pl.when(pl.program_id(2) == 0)
    def _():
        h_scr[...] = (x_ref[0] * (1.0 + sc_ref[0]) + sh_ref[0]).astype(_BF16)

    o_ref[0] = jnp.dot(h_scr[...], w_ref[...], preferred_element_type=_F32)


def in_projection(x, sc, sh, w, *, tm=1024, tn=512):
    b, t, d = x.shape
    n = w.shape[1]
    tm = min(tm, t)
    return pl.pallas_call(
        _inproj_kernel,
        grid=(b, t // tm, n // tn),
        in_specs=[pl.BlockSpec((1, tm, d), lambda i, j, k: (i, j, 0)),
                  pl.BlockSpec((1, 1, d), lambda i, j, k: (i, 0, 0)),
                  pl.BlockSpec((1, 1, d), lambda i, j, k: (i, 0, 0)),
                  pl.BlockSpec((d, tn), lambda i, j, k: (0, k))],
        out_specs=pl.BlockSpec((1, tm, tn), lambda i, j, k: (i, j, k)),
        out_shape=jax.ShapeDtypeStruct((b, t, n), _F32),
        scratch_shapes=[pltpu.VMEM((tm, d), _BF16)],
        compiler_params=pltpu.CompilerParams(
            dimension_semantics=("parallel", "parallel", "arbitrary"), vmem_limit_bytes=VMEM_LIMIT),
        name="in_projection",
    )(x, sc, sh, w)


def _lmm_kernel(w_ref, a_ref, o_ref):
    o_ref[0] = _dot(w_ref[...], a_ref[0], True)


def left_matmul(w, a, *, tn=2048):
    mw, k = w.shape
    g, _, n = a.shape
    tn = min(tn, n)
    return pl.pallas_call(
        _lmm_kernel,
        grid=(g, n // tn),
        in_specs=[pl.BlockSpec((mw, k), lambda i, j: (0, 0)),
                  pl.BlockSpec((1, k, tn), lambda i, j: (i, 0, j))],
        out_specs=pl.BlockSpec((1, mw, tn), lambda i, j: (i, 0, j)),
        out_shape=jax.ShapeDtypeStruct((g, mw, n), _F32),
        compiler_params=pltpu.CompilerParams(
            dimension_semantics=("parallel", "parallel"), vmem_limit_bytes=VMEM_LIMIT),
    )(w, a)


def _dft_cos_sin(n):
    idx = np.arange(n)
    ang = 2.0 * np.pi * ((idx[:, None] * idx[None, :]) % n) / n
    return np.cos(ang), np.sin(ang)


def fourier_mix(u):
    b, t, _ = u.shape
    gd = FOURIER_GROUP_DIM
    cc, sc = _dft_cos_sin(gd)
    w_c = jnp.asarray(np.concatenate([cc, sc], axis=1) / np.sqrt(gd), _F32)
    pq = matmul(u.reshape(b * t * N_FOURIER_GROUPS, gd), w_c, precise=True, tm=2048, tn=2 * gd)
    pq = pq.reshape(b, t, N_FOURIER_GROUPS, 2, gd)
    if t <= 512:
        ct, st = _dft_cos_sin(t)
        w_t = jnp.asarray(np.concatenate([ct, st], axis=0) / np.sqrt(t), _F32)
        m = left_matmul(w_t, pq.reshape(b, t, -1))
        m = m.reshape(b, 2, t, N_FOURIER_GROUPS, 2, gd)
        out = m[:, 0, :, :, 0] - m[:, 1, :, :, 1]
        return out.reshape(b, t, D_FOURIER)
    n1 = int(round(np.sqrt(t)))
    assert n1 * n1 == t
    c1, s1 = _dft_cos_sin(n1)
    w1 = jnp.asarray(np.concatenate([c1, s1], axis=0) / np.sqrt(n1), _F32)
    m1 = left_matmul(w1, pq.reshape(b, n1, -1))
    m1 = m1.reshape(b, 2, n1, n1, N_FOURIER_GROUPS, 2, gd)
    x1r = m1[:, 0, :, :, :, 0] - m1[:, 1, :, :, :, 1]
    x1i = -(m1[:, 0, :, :, :, 1] + m1[:, 1, :, :, :, 0])
    tw = 2.0 * np.pi * (np.arange(n1)[:, None] * np.arange(n1)[None, :]) / t
    twc = jnp.asarray(np.cos(tw), _F32)[None, :, :, None, None]
    tws = jnp.asarray(np.sin(tw), _F32)[None, :, :, None, None]
    ar = x1r * twc + x1i * tws
    ai = x1i * twc - x1r * tws
    a2 = jnp.stack([ar, ai], axis=2).reshape(b * n1, 2 * n1, D_FOURIER)
    w2 = jnp.asarray(np.concatenate([c1, s1], axis=1) / np.sqrt(n1), _F32)
    o = left_matmul(w2, a2)
    o = o.reshape(b, n1, n1, D_FOURIER).transpose(0, 2, 1, 3)
    return o.reshape(b, t, D_FOURIER)


N_LEVELS = int(np.log2(CHUNK))
SCAN_PASSES = {"gram": 1, "tinv": 1, "apply": 1, "state": 1}

_NN = (((1,), (0,)), ((), ()))
_NT = (((1,), (1,)), ((), ()))
_TN = (((0,), (0,)), ((), ()))

_C_INCL, _C_STRICT, _C_EYE, _C_LEVEL0 = 0, 1, 2, 3
_C_LEFT = _C_LEVEL0 + N_LEVELS
_C_RIGHT = _C_LEFT + 1


def _scan_consts(reverse):
    L = CHUNK
    t = np.arange(L)[:, None]
    s = np.arange(L)[None, :]
    if reverse:
        t, s = s, t
    strict = (s < t)
    incl = (s <= t)
    levels = []
    bs = 1
    while bs < L:
        levels.append((t // (2 * bs) == s // (2 * bs)) & (t % (2 * bs) >= bs) & (s % (2 * bs) < bs))
        bs *= 2
    pair = lambda m: np.concatenate([m, m], axis=1)
    left = np.concatenate([np.ones((L, HEAD_DIM)), np.zeros((L, HEAD_DIM))], axis=1)
    slabs = [pair(incl), pair(strict), pair(np.eye(L))] + [pair(l) for l in levels] + [left, 1.0 - left]
    return np.stack(slabs).astype(np.float32)


def _pdot(a, b, dims, passes):
    dg = lambda x, y: lax.dot_general(x, y, dims, preferred_element_type=_F32)
    ah = a.astype(_BF16)
    bh = b.astype(_BF16)
    if passes == 1:
        return dg(ah, bh)
    al = (a - ah.astype(_F32)).astype(_BF16)
    bl = (b - bh.astype(_F32)).astype(_BF16)
    return dg(ah, bh) + dg(ah, bl) + dg(al, bh)


def _wkv_kernel(c_ref, r_ref, k_ref, v_ref, rp_ref, kp_ref, vp_ref, rn_ref, kn_ref, vn_ref,
                cwr_ref, cwk_ref, cwv_ref, wdn_ref, adn_ref, wup_ref, aup_ref, w0_ref, a0_ref,
                kkw_ref, kaw_ref, rkw_ref, s0_ref, y_ref, bo_ref, st_ref, s_scr, *, nsub, npp, reverse):
    L, N = CHUNK, HEAD_DIM
    tb = nsub * L
    j = pl.program_id(2)
    nblk = pl.num_programs(2)
    p_gram, p_tinv, p_apply, p_state = (SCAN_PASSES[k] for k in ("gram", "tinv", "apply", "state"))
    inclp = c_ref[_C_INCL]
    strictp = c_ref[_C_STRICT]
    left = c_ref[_C_LEFT]
    right = c_ref[_C_RIGHT]
    bdmask = jnp.concatenate([left, right], axis=0)
    bdmask_b = bdmask.astype(_BF16)
    incl_b = inclp[:, :L].astype(_BF16)

    tblk = (nblk - 1 - j) if reverse else j
    has_prev = (tblk > 0).astype(_F32)
    has_next = (tblk < nblk - 1).astype(_F32)
    row = lax.broadcasted_iota(jnp.int32, (tb, 1), 0)

    def conv(u_ref, up_ref, un_ref, cw_ref):
        u = u_ref[0]
        before = jnp.where(row == 0, up_ref[0, 7:8, :] * has_prev, pltpu.roll(u, 1, 0))
        after = jnp.where(row == tb - 1, un_ref[0, 0:1, :] * has_next, pltpu.roll(u, tb - 1, 0))
        return before * cw_ref[0:1, :] + u * cw_ref[1:2, :] + after * cw_ref[2:3, :]

    def segsum(x):
        x_h = x.astype(_BF16)
        x_l = (x - x_h.astype(_F32)).astype(_BF16)
        return (jnp.dot(x_h, bdmask_b, preferred_element_type=_F32)
                + jnp.dot(x_l, bdmask_b, preferred_element_type=_F32))

    r_all = conv(r_ref, rp_ref, rn_ref, cwr_ref)
    k_all = conv(k_ref, kp_ref, kn_ref, cwk_ref)
    v_all = conv(v_ref, vp_ref, vn_ref, cwv_ref)
    w_up = jnp.dot(jnp.tanh(wdn_ref[0]).astype(_BF16), wup_ref[...].astype(_BF16), preferred_element_type=_F32)
    a_up = jnp.dot(adn_ref[0].astype(_BF16), aup_ref[...].astype(_BF16), preferred_element_type=_F32)
    z = -(w0_ref[...] + w_up)
    softplus = jnp.maximum(z, 0.0) + jnp.log(1.0 + jnp.exp(-jnp.abs(z)))
    lw_all = -jnp.exp(-softplus - 0.5)
    a_rate = jax.nn.sigmoid(a0_ref[...] + a_up)
    kd_all = k_all * (1.0 + (a_rate - 1.0) * kaw_ref[...])
    kkr = k_all * kkw_ref[...]
    rkd = r_all * kd_all * rkw_ref[...]
    kk_parts, bonus_parts = [], []
    for q in range(npp):
        sl = slice(q * LANES, (q + 1) * LANES)
        kq = kkr[:, sl]
        kk_parts.append(kq * lax.rsqrt(segsum(kq * kq) + 1e-12))
        bonus_parts.append(segsum(rkd[:, sl]) * v_all[:, sl])
    bo_ref[0] = jnp.concatenate(bonus_parts, axis=1)
    kk_all = jnp.concatenate(kk_parts, axis=1)
    bb_all = kk_all * a_rate

    def bd(x):
        return jnp.concatenate([x * left, x * right], axis=0)

    @pl.when(j == 0)
    def _():
        zero = jnp.zeros((N, N), _F32)
        for q in range(npp):
            top = jnp.concatenate([s0_ref[0, 2 * q], zero], axis=1)
            bot = jnp.concatenate([zero, s0_ref[0, 2 * q + 1]], axis=1)
            s_scr[q] = jnp.concatenate([top, bot], axis=0)

    order = range(nsub - 1, -1, -1) if reverse else range(nsub)
    last_row = 0 if reverse else L - 1

    chains = [(q, i) for i in order for q in range(npp)]

    def each(fn, *lists):
        return [fn(*args) for args in zip(*lists)]

    def tile(x, c):
        q, i = c
        return x[i * L:(i + 1) * L, q * LANES:(q + 1) * LANES]

    lw = [tile(lw_all, c) for c in chains]

    def cumulative(x):
        x_h = x.astype(_BF16)
        rem = x - x_h.astype(_F32)
        x_m = rem.astype(_BF16)
        x_l = (rem - x_m.astype(_F32)).astype(_BF16)
        return (jnp.dot(incl_b, x_h, preferred_element_type=_F32)
                + jnp.dot(incl_b, x_m, preferred_element_type=_F32)
                + jnp.dot(incl_b, x_l, preferred_element_type=_F32))

    cum = each(cumulative, lw)
    tot = [x[last_row:last_row + 1, :] for x in cum]
    g_iv = [jnp.exp(-x) for x in cum]
    g_rem = each(lambda t_, x: jnp.exp(t_ - x), tot, cum)
    v = [tile(v_all, c) for c in chains]
    rt = [tile(r_all, c) * jnp.exp(x) for c, x in zip(chains, cum)]
    at = [-tile(kk_all, c) * jnp.exp(x - w) for c, x, w in zip(chains, cum, lw)]
    kd = [tile(kd_all, c) for c in chains]
    bb = [tile(bb_all, c) for c in chains]
    bh = each(lambda x, g: x * g, bb, g_rem)
    kh = each(lambda x, g: x * g, kd, g_rem)
    gram = each(lambda a, r_, b_, k_, g: _pdot(jnp.concatenate([a, r_], axis=0),
                                               jnp.concatenate([bd(b_ * g), bd(k_ * g)], axis=0), _NT, p_gram),
                at, rt, bb, kd, g_iv)
    a_ab = [g[:L, :LANES] * strictp for g in gram]
    a_ak = [g[:L, LANES:] * strictp for g in gram]
    a_rb = [g[L:, :LANES] * inclp for g in gram]
    a_rk = [g[L:, LANES:] * inclp for g in gram]
    tinv = [c_ref[_C_EYE] + a * c_ref[_C_LEVEL0] for a in a_ab]
    for lv in range(1, N_LEVELS):
        tx = each(lambda t_, a: _pdot(t_, bd(a * c_ref[_C_LEVEL0 + lv]), _NN, p_tinv), tinv, a_ab)
        tinv = each(lambda t_, x: t_ + _pdot(x, bd(t_), _NN, p_tinv), tinv, tx)
    wt = each(lambda t_, a: _pdot(t_, bd(a), _NN, p_apply), tinv, at)
    av = each(lambda a, v_: _pdot(a, bd(v_), _NN, p_apply), a_ak, v)
    u0 = each(lambda t_, x: _pdot(t_, bd(x), _NN, p_apply), tinv, av)
    qq = each(lambda r_, a, w: r_ + _pdot(a, bd(w), _NN, p_apply), rt, a_rb, wt)
    y0 = each(lambda ab, ak, u, v_: _pdot(jnp.concatenate([ab, ak], axis=1),
                                          jnp.concatenate([bd(u), bd(v_)], axis=0), _NN, p_apply),
              a_rb, a_rk, u0, v)
    pm = each(lambda w, b_: _pdot(w, b_, _TN, p_state) * bdmask, wt, bh)
    cp = each(lambda u, v_, b_, k_: _pdot(jnp.concatenate([u, v_], axis=0), jnp.concatenate([b_, k_], axis=0),
                                          _TN, p_state) * bdmask, u0, v, bh, kh)
    gl = [jnp.exp(x) for x in tot]

    for n, (q, i) in enumerate(chains):
        s0 = s_scr[q]
        y_ref[0, i * L:(i + 1) * L, q * LANES:(q + 1) * LANES] = _pdot(qq[n], s0, _NT, p_state) + y0[n]
        s_scr[q] = s0 * gl[n] + _pdot(s0, pm[n], _NN, p_state) + cp[n]

    @pl.when(j == pl.num_programs(2) - 1)
    def _():
        for q in range(npp):
            s = s_scr[q]
            st_ref[0, 2 * q] = s[:N, :N]
            st_ref[0, 2 * q + 1] = s[N:, N:]


def wkv_scan(proj, p, di, s0, reverse):
    b, t, _ = proj.shape
    d = D_RWKV
    nsub = CHUNKS_PER_STEP
    npp = PAIRS_PER_STEP
    tb = nsub * CHUNK
    width = LANES * npp
    nblk = t // tb
    ngrp = d // width
    halo = 8
    consts = jnp.asarray(_scan_consts(reverse))
    pad_rows = lambda w: jnp.zeros((LANES, d), _F32).at[di * w.shape[0]:(di + 1) * w.shape[0]].set(w)
    wup = pad_rows(p["w_decay_up"][di])
    aup = pad_rows(p["a_up"][di])
    row = lambda a: a.reshape(1, d)

    tmap = (lambda j: nblk - 1 - j) if reverse else (lambda j: j)

    def seq(col0):
        return pl.BlockSpec((1, tb, width), lambda i, g, j: (i, tmap(j), col0 // width + g))

    def prev(col0):
        return pl.BlockSpec((1, halo, width),
                            lambda i, g, j: (i, jnp.maximum(tmap(j) * (tb // halo) - 1, 0), col0 // width + g))

    def nxt(col0):
        return pl.BlockSpec((1, halo, width),
                            lambda i, g, j: (i, jnp.minimum((tmap(j) + 1) * (tb // halo), t // halo - 1),
                                             col0 // width + g))

    def lowrank(col0):
        return pl.BlockSpec((1, tb, LANES), lambda i, g, j: (i, tmap(j), col0 // LANES))

    def cols(nrows, col0=0):
        return pl.BlockSpec((nrows, width), lambda i, g, j: (0, col0 // width + g))

    out_seq = pl.BlockSpec((1, tb, width), lambda i, g, j: (i, tmap(j), g))
    st_spec = pl.BlockSpec((1, 2 * npp, HEAD_DIM, HEAD_DIM), lambda i, g, j: (i, g, 0, 0))
    in_specs = ([pl.BlockSpec(consts.shape, lambda i, g, j: (0, 0, 0))]
                + [seq(P_R), seq(P_K), seq(P_V), prev(P_R), prev(P_K), prev(P_V), nxt(P_R), nxt(P_K), nxt(P_V)]
                + [cols(SHORT_CONV, 0), cols(SHORT_CONV, d), cols(SHORT_CONV, 2 * d)]
                + [lowrank(P_WDN), lowrank(P_ADN), cols(LANES), cols(LANES)]
                + [cols(1)] * 5 + [st_spec])
    y, bonus, st = pl.pallas_call(
        functools.partial(_wkv_kernel, nsub=nsub, npp=npp, reverse=reverse),
        grid=(b, ngrp, nblk),
        in_specs=in_specs,
        out_specs=[out_seq, out_seq, st_spec],
        out_shape=[jax.ShapeDtypeStruct((b, t, d), _F32),
                   jax.ShapeDtypeStruct((b, t, d), _F32),
                   jax.ShapeDtypeStruct(s0.shape, _F32)],
        scratch_shapes=[pltpu.VMEM((npp, LANES, LANES), _F32)],
        compiler_params=pltpu.CompilerParams(
            dimension_semantics=("parallel", "parallel", "arbitrary"), vmem_limit_bytes=VMEM_LIMIT),
        name="wkv_scan_bwd" if reverse else "wkv_scan_fwd",
    )(consts, proj, proj, proj, proj, proj, proj, proj, proj, proj,
      p["conv_w"], p["conv_w"], p["conv_w"], proj, proj, wup, aup,
      row(p["w_decay0"][di]), row(p["a0"][di]), row(p["k_k"]), row(p["k_a"]), row(p["r_k"]), s0)
    return y, bonus, st


def _layer_norm_rows(x, g, b):
    mu = jnp.mean(x, axis=-1, keepdims=True)
    xc = x - mu
    var = jnp.mean(xc * xc, axis=-1, keepdims=True)
    return xc * lax.rsqrt(var + LN_EPS) * g + b


def _post_kernel(yf_ref, yb_ref, bf_ref, bb_ref, gdn_ref, gf_ref, gr_ref, fo_ref, x_ref,
                 g1_ref, sc2_ref, sh2_ref, gup_ref, gng_ref, gnb_ref, wfo_ref, wro_ref, wout_ref,
                 l1g_ref, l1b_ref, wr_ref, bd_ref, x1_ref, h2_ref, lg_ref):
    bdm = bd_ref[...]

    def segsum(x):
        x_h = x.astype(_BF16)
        x_l = (x - x_h.astype(_F32)).astype(_BF16)
        return jnp.dot(x_h, bdm, preferred_element_type=_F32) + jnp.dot(x_l, bdm, preferred_element_type=_F32)

    y = yf_ref[0] + yb_ref[0]
    parts = []
    for q in range(D_RWKV // LANES):
        yq = y[:, q * LANES:(q + 1) * LANES]
        dq = yq - segsum(yq) * (1.0 / HEAD_DIM)
        parts.append(dq * lax.rsqrt(segsum(dq * dq) * (1.0 / HEAD_DIM) + GN_EPS))
    yn = jnp.concatenate(parts, axis=1) * gng_ref[...] + gnb_ref[...]
    g = jnp.dot(jax.nn.sigmoid(gdn_ref[0]).astype(_BF16), gup_ref[...], preferred_element_type=_F32)
    z = (yn + bf_ref[0] + bb_ref[0]) * g
    o_r = jnp.dot(z.astype(_BF16), wro_ref[...], preferred_element_type=_F32)
    o_f = jnp.dot(fo_ref[0].astype(_BF16), wfo_ref[...], preferred_element_type=_F32)
    merged = jax.nn.sigmoid(gf_ref[0]) * o_f + jax.nn.sigmoid(gr_ref[0]) * o_r
    mix = jnp.dot(merged.astype(_BF16), wout_ref[...], preferred_element_type=_F32)
    x1 = _layer_norm_rows(DEEPNORM_ALPHA * x_ref[0] + g1_ref[0] * mix, l1g_ref[...], l1b_ref[...])
    x1_ref[0] = x1
    h2 = x1 * (1.0 + sc2_ref[0]) + sh2_ref[0]
    h2_ref[0] = h2.astype(_BF16)
    lg_ref[0] = jnp.dot(h2, wr_ref[...], precision=_HI, preferred_element_type=_F32)


def post_mix(y_f, y_b, bo_f, bo_b, proj, fourier, x, g1, sc2, sh2, p, *, tm=256):
    b, t, d = x.shape
    tm = min(tm, t)
    bdm = jnp.asarray(np.kron(np.eye(2), np.ones((HEAD_DIM, HEAD_DIM))), _BF16)
    tok = lambda w, c0=0: pl.BlockSpec((1, tm, w), lambda i, j: (i, j, c0 // w))
    per_seq = pl.BlockSpec((1, 1, d), lambda i, j: (i, 0, 0))
    full = lambda a: pl.BlockSpec(a.shape, lambda i, j: (0,) * a.ndim)
    row = lambda a: a.reshape(1, -1)
    weights = [p["g_up"].astype(_BF16), row(p["gn_g"]), row(p["gn_b"]), p["w_fo"].astype(_BF16),
               p["w_ro"].astype(_BF16), p["w_out"].astype(_BF16), row(p["ln1_g"]), row(p["ln1_b"]),
               p["w_router"], bdm]
    return pl.pallas_call(
        _post_kernel,
        grid=(b, t // tm),
        in_specs=[tok(d), tok(d), tok(d), tok(d), tok(LORA_G, P_GDN), tok(d, P_GATE_F), tok(d, P_GATE_R),
                  tok(D_FOURIER), tok(d), per_seq, per_seq, per_seq] + [full(w) for w in weights],
        out_specs=[tok(d), tok(d), tok(N_EXPERTS)],
        out_shape=[jax.ShapeDtypeStruct((b, t, d), _F32),
                   jax.ShapeDtypeStruct((b, t, d), _BF16),
                   jax.ShapeDtypeStruct((b, t, N_EXPERTS), _F32)],
        compiler_params=pltpu.CompilerParams(
            dimension_semantics=("parallel", "parallel"), vmem_limit_bytes=VMEM_LIMIT),
        name="post_mix",
    )(y_f, y_b, bo_f, bo_b, proj, proj, proj, fourier, x, g1, sc2, sh2, *weights)


def _expert_kernel(x_ref, g_ref, w1_ref, w3_ref, w2_ref, o_ref):
    f = pl.program_id(2)
    x = x_ref[0].astype(_BF16)
    h1 = jnp.dot(x, w1_ref[0].astype(_BF16), preferred_element_type=_F32)
    h3 = jnp.dot(x, w3_ref[0].astype(_BF16), preferred_element_type=_F32)
    he = (h1 * jax.nn.sigmoid(h1)) * h3
    part = jnp.dot(he.astype(_BF16), w2_ref[0].astype(_BF16), preferred_element_type=_F32)

    @pl.when(f == 0)
    def _():
        o_ref[0] = part

    @pl.when(f > 0)
    def _():
        o_ref[0] += part

    @pl.when(f == pl.num_programs(2) - 1)
    def _():
        o_ref[0] = o_ref[0] * g_ref[0]


def expert_ffn(xe, gate, w1, w3, w2, *, tm=1024, tf=512):
    e, c, d = xe.shape
    f = w1.shape[2]
    tm = min(tm, c)
    return pl.pallas_call(
        _expert_kernel,
        grid=(e, c // tm, f // tf),
        in_specs=[pl.BlockSpec((1, tm, d), lambda i, j, k: (i, j, 0)),
                  pl.BlockSpec((1, tm, 1), lambda i, j, k: (i, j, 0)),
                  pl.BlockSpec((1, d, tf), lambda i, j, k: (i, 0, k)),
                  pl.BlockSpec((1, d, tf), lambda i, j, k: (i, 0, k)),
                  pl.BlockSpec((1, tf, d), lambda i, j, k: (i, k, 0))],
        out_specs=pl.BlockSpec((1, tm, d), lambda i, j, k: (i, j, 0)),
        out_shape=jax.ShapeDtypeStruct((e, c, d), _F32),
        compiler_params=pltpu.CompilerParams(
            dimension_semantics=("parallel", "parallel", "arbitrary"), vmem_limit_bytes=VMEM_LIMIT),
        name="expert_ffn",
    )(xe, gate, w1, w3, w2)


def _layer_norm(x, g, b):
    mu = jnp.mean(x, axis=-1, keepdims=True)
    var = jnp.mean(jnp.square(x - mu), axis=-1, keepdims=True)
    return (x - mu) * lax.rsqrt(var + LN_EPS) * g + b


def _expert_choice_ffn(h2, logits, p):
    b, t, d = h2.shape
    n = b * t
    cap = CAPACITY_FACTOR * n // N_EXPERTS
    hf = h2.reshape(n, d)
    aff = jax.nn.softmax(logits.reshape(n, N_EXPERTS), axis=-1)
    gate, idx = lax.top_k(aff.T, cap)
    xe = hf[idx]
    ye = expert_ffn(xe, gate[..., None], p["w_e1"], p["w_e3"], p["w_e2"])
    out = jnp.zeros((n, d), _F32).at[idx.reshape(-1)].add(ye.reshape(-1, d))
    return out.reshape(b, t, d)


def _trunk_layer(x, cond, s_f0, s_b0, p):
    mod = matmul(jax.nn.silu(cond), p["w_ada"], precise=True) + p["b_ada"]
    sh1, sc1, g1, sh2, sc2, g2 = [a[:, None, :] for a in jnp.split(mod, 6, axis=-1)]
    proj = in_projection(x, sc1, sh1, p["w_in_perm"])
    fourier = fourier_mix(proj[:, :, P_UF:P_UF + D_FOURIER])
    y_f, bo_f, s_f = wkv_scan(proj, p, 0, s_f0, False)
    y_b, bo_b, s_b = wkv_scan(proj, p, 1, s_b0, True)
    x1, h2, logits = post_mix(y_f, y_b, bo_f, bo_b, proj, fourier, x, g1, sc2, sh2, p)
    x2 = _layer_norm(DEEPNORM_ALPHA * x1 + g2 * _expert_choice_ffn(h2, logits, p), p["ln2_g"], p["ln2_b"])
    return x2, s_f, s_b


def kernel(x_prompt, x_sample, state_fwd, state_bwd, c, c_ctx, w_ada, b_ada, w_in, conv_w, w_decay0, w_decay_up, a0, a_up, g_up, k_k, k_a, r_k, gn_g, gn_b, w_fo, w_ro, w_out, ln1_g, ln1_b, w_router, w_e1, w_e3, w_e2, ln2_g, ln2_b):
    n_ctx = x_prompt.shape[0]
    cond_ctx = jnp.broadcast_to(c_ctx, (n_ctx, D_MODEL))
    zero_state = jnp.zeros((n_ctx, N_HEADS, HEAD_DIM, HEAD_DIM), _F32)
    names = ("w_ada", "b_ada", "w_in", "conv_w", "w_decay0", "w_decay_up", "a0", "a_up", "g_up", "k_k", "k_a",
             "r_k", "gn_g", "gn_b", "w_fo", "w_ro", "w_out", "ln1_g", "ln1_b", "w_router", "w_e1", "w_e3",
             "w_e2", "ln2_g", "ln2_b")
    stacked = (w_ada, b_ada, w_in, conv_w, w_decay0, w_decay_up, a0, a_up, g_up, k_k, k_a, r_k, gn_g, gn_b,
               w_fo, w_ro, w_out, ln1_g, ln1_b, w_router, w_e1, w_e3, w_e2, ln2_g, ln2_b)
    xp, xs = x_prompt, x_sample
    new_f, new_b = [], []
    for l in range(DEPTH):
        p = {k: a[l] for k, a in zip(names, stacked)}
        p["w_in_perm"] = _permute_in_weight(p["w_in"]).astype(_BF16)
        xp, s_f, s_b = _trunk_layer(xp, cond_ctx, zero_state, zero_state, p)
        new_f.append(s_f)
        new_b.append(s_b)
        xs, _, _ = _trunk_layer(xs, c, state_fwd[:, l], state_bwd[:, l], p)
    return (xp, xs, jnp.stack(new_f, axis=1), jnp.stack(new_b, axis=1))
```

```python
import functools

import numpy as np
import jax
import jax.numpy as jnp
from jax import lax
from jax.experimental import pallas as pl
from jax.experimental.pallas import tpu as pltpu

D_MODEL = 1024
N_FOURIER_GROUPS = 4
FOURIER_GROUP_DIM = 128
D_FOURIER = N_FOURIER_GROUPS * FOURIER_GROUP_DIM
HEAD_DIM = 64
D_RWKV = D_MODEL
N_HEADS = D_RWKV // HEAD_DIM
LORA_W = 64
LORA_A = 64
LORA_G = 128
N_DIR = 2
SHORT_CONV = 3
GN_EPS = 64e-5
N_EXPERTS = 16
CAPACITY_FACTOR = 2
D_EXPERT = 2048
LN_EPS = 1e-5
DEPTH = 1
DEEPNORM_ALPHA = (2.0 * DEPTH) ** 0.25
IN_WIDTHS = (D_FOURIER, 3 * D_RWKV, N_DIR * LORA_W, N_DIR * LORA_A, LORA_G, D_MODEL, D_MODEL)
D_IN = sum(IN_WIDTHS)
SPLIT_POINTS = tuple(sum(IN_WIDTHS[:i + 1]) for i in range(len(IN_WIDTHS) - 1))

LANES = 128
CHUNK = HEAD_DIM
CHUNKS_PER_STEP = 4
PAIRS_PER_STEP = 4
VMEM_LIMIT = 48 * 1024 * 1024

_HI = lax.Precision.HIGHEST
_F32 = jnp.float32
_BF16 = jnp.bfloat16


def _dot(a, b, precise):
    if precise:
        return jnp.dot(a, b, precision=_HI, preferred_element_type=_F32)
    return jnp.dot(a.astype(_BF16), b.astype(_BF16), preferred_element_type=_F32)


def _mm_kernel(a_ref, b_ref, o_ref, *, precise):
    o_ref[...] = _dot(a_ref[...], b_ref[...], precise)


def matmul(a, b, *, precise=False, tm=512, tn=512):
    m, k = a.shape
    n = b.shape[1]
    tm = min(tm, m)
    tn = min(tn, n)
    return pl.pallas_call(
        functools.partial(_mm_kernel, precise=precise),
        grid=(pl.cdiv(m, tm), pl.cdiv(n, tn)),
        in_specs=[pl.BlockSpec((tm, k), lambda i, j: (i, 0)),
                  pl.BlockSpec((k, tn), lambda i, j: (0, j))],
        out_specs=pl.BlockSpec((tm, tn), lambda i, j: (i, j)),
        out_shape=jax.ShapeDtypeStruct((m, n), _F32),
        compiler_params=pltpu.CompilerParams(
            dimension_semantics=("parallel", "parallel"), vmem_limit_bytes=VMEM_LIMIT),
    )(a, b)


P_R, P_K, P_V = 0, D_RWKV, 2 * D_RWKV
P_GATE_F = 3 * D_RWKV
P_GATE_R = P_GATE_F + D_MODEL
P_UF = P_GATE_R + D_MODEL
P_WDN = P_UF + D_FOURIER
P_ADN = P_WDN + N_DIR * LORA_W
P_GDN = P_ADN + N_DIR * LORA_A
D_PROJ = P_GDN + LORA_G + 128
assert D_PROJ % 512 == 0 and N_DIR * LORA_W == LANES and N_DIR * LORA_A == LANES and LORA_G == LANES


def _permute_in_weight(w_in):
    s = (0,) + SPLIT_POINTS + (D_IN,)
    seg = [w_in[:, s[i]:s[i + 1]] for i in range(len(IN_WIDTHS))]
    pad = jnp.zeros((w_in.shape[0], D_PROJ - D_IN), w_in.dtype)
    return jnp.concatenate([seg[1], seg[5], seg[6], seg[0], seg[2], seg[3], seg[4], pad], axis=1)


def _inproj_kernel(x_ref, sc_ref, sh_ref, w_ref, o_ref, h_scr):
    @pl.when(pl.program_id(2) == 0)
    def _():
        h_scr[...] = (x_ref[0] * (1.0 + sc_ref[0]) + sh_ref[0]).astype(_BF16)

    o_ref[0] = jnp.dot(h_scr[...], w_ref[...], preferred_element_type=_F32)


def in_projection(x, sc, sh, w, *, tm=1024, tn=512):
    b, t, d = x.shape
    n = w.shape[1]
    tm = min(tm, t)
    return pl.pallas_call(
        _inproj_kernel,
        grid=(b, t // tm, n // tn),
        in_specs=[pl.BlockSpec((1, tm, d), lambda i, j, k: (i, j, 0)),
                  pl.BlockSpec((1, 1, d), lambda i, j, k: (i, 0, 0)),
                  pl.BlockSpec((1, 1, d), lambda i, j, k: (i, 0, 0)),
                  pl.BlockSpec((d, tn), lambda i, j, k: (0, k))],
        out_specs=pl.BlockSpec((1, tm, tn), lambda i, j, k: (i, j, k)),
        out_shape=jax.ShapeDtypeStruct((b, t, n), _F32),
        scratch_shapes=[pltpu.VMEM((tm, d), _BF16)],
        compiler_params=pltpu.CompilerParams(
            dimension_semantics=("parallel", "parallel", "arbitrary"), vmem_limit_bytes=VMEM_LIMIT),
        name="in_projection",
    )(x, sc, sh, w)


FOURIER_UNROLL = 8


def _dft_cos_sin(n):
    idx = np.arange(n)
    ang = 2.0 * np.pi * ((idx[:, None] * idx[None, :]) % n) / n
    return np.cos(ang), np.sin(ang)


def _split(x):
    x_h = x.astype(_BF16)
    return x_h, (x - x_h.astype(_F32)).astype(_BF16)


def _dot3(a, b):
    dg = lambda x, y: jnp.dot(x, y, preferred_element_type=_F32)
    return dg(a[0], b[0]) + dg(a[0], b[1]) + dg(a[1], b[0])


def _fourier_dense_kernel(u_ref, wc_ref, wt_ref, o_ref):
    t, gd = u_ref.shape[1], FOURIER_GROUP_DIM
    z = _dot3(_split(u_ref[0]), _split(wc_ref[...]))
    m = _dot3(_split(wt_ref[...]), _split(z))
    o_ref[0] = m[:t, :gd] - m[t:, gd:]


def _fourier_factored_kernel(u_ref, wc_ref, w1_ref, twc_ref, tws_ref, w2_ref, o_ref, p_scr, q_scr, a_scr,
                             *, n1):
    gd = FOURIER_GROUP_DIM
    z = _dot3(_split(u_ref[0]), _split(wc_ref[...]))
    p_scr[...] = z[:, :gd]
    q_scr[...] = z[:, gd:]
    w1 = _split(w1_ref[...])
    w2 = _split(w2_ref[...])

    def stage1(step, carry):
        t2s = [step * FOURIER_UNROLL + u for u in range(FOURIER_UNROLL)]
        rows = [pl.ds(t2, n1, stride=n1) for t2 in t2s]
        ms = [_dot3(w1, _split(jnp.concatenate([p_scr[r, :], q_scr[r, :]], axis=1))) for r in rows]
        for t2, m in zip(t2s, ms):
            x1r = m[:n1, :gd] - m[n1:, gd:]
            x1i = -(m[:n1, gd:] + m[n1:, :gd])
            c = twc_ref[t2]
            s = tws_ref[t2]
            a_scr[pl.ds(t2, n1, stride=2 * n1), :] = x1r * c + x1i * s
            a_scr[pl.ds(n1 + t2, n1, stride=2 * n1), :] = x1i * c - x1r * s
        return carry

    lax.fori_loop(0, n1 // FOURIER_UNROLL, stage1, 0)

    def stage2(step, carry):
        t1s = [step * FOURIER_UNROLL + u for u in range(FOURIER_UNROLL)]
        outs = [_dot3(w2, _split(a_scr[pl.ds(pl.multiple_of(t1 * 2 * n1, 2 * n1), 2 * n1), :])) for t1 in t1s]
        for t1, o in zip(t1s, outs):
            o_ref[0, pl.ds(t1, n1, stride=n1), :] = o
        return carry

    lax.fori_loop(0, n1 // FOURIER_UNROLL, stage2, 0)


def fourier_mix(proj):
    b, t, _ = proj.shape
    gd = FOURIER_GROUP_DIM
    cc, sc = _dft_cos_sin(gd)
    w_c = jnp.asarray(np.concatenate([cc, sc], axis=1) / np.sqrt(gd), _F32)
    u_spec = pl.BlockSpec((1, t, gd), lambda i, g: (i, 0, P_UF // gd + g))
    o_spec = pl.BlockSpec((1, t, gd), lambda i, g: (i, 0, g))
    full = lambda a: pl.BlockSpec(a.shape, lambda i, g: (0,) * a.ndim)
    params = pltpu.CompilerParams(dimension_semantics=("parallel", "parallel"), vmem_limit_bytes=VMEM_LIMIT)
    out_shape = jax.ShapeDtypeStruct((b, t, D_FOURIER), _F32)
    if t <= 512:
        ct, st = _dft_cos_sin(t)
        w_t = jnp.asarray(np.concatenate([ct, st], axis=0) / np.sqrt(t), _F32)
        return pl.pallas_call(
            _fourier_dense_kernel, grid=(b, N_FOURIER_GROUPS),
            in_specs=[u_spec, full(w_c), full(w_t)], out_specs=o_spec, out_shape=out_shape,
            compiler_params=params, name="fourier_dense",
        )(proj, w_c, w_t)
    n1 = int(round(np.sqrt(t)))
    assert n1 * n1 == t and n1 % FOURIER_UNROLL == 0
    c1, s1 = _dft_cos_sin(n1)
    w1 = jnp.asarray(np.concatenate([c1, s1], axis=0) / np.sqrt(n1), _F32)
    w2 = jnp.asarray(np.concatenate([c1, s1], axis=1) / np.sqrt(n1), _F32)
    tw = 2.0 * np.pi * (np.arange(n1)[:, None] * np.arange(n1)[None, :]) / t
    lanes = np.ones((1, 1, gd))
    twc = jnp.asarray(np.cos(tw)[:, :, None] * lanes, _F32)
    tws = jnp.asarray(np.sin(tw)[:, :, None] * lanes, _F32)
    return pl.pallas_call(
        functools.partial(_fourier_factored_kernel, n1=n1), grid=(b, N_FOURIER_GROUPS),
        in_specs=[u_spec, full(w_c), full(w1), full(twc), full(tws), full(w2)],
        out_specs=o_spec, out_shape=out_shape,
        scratch_shapes=[pltpu.VMEM((t, gd), _F32), pltpu.VMEM((t, gd), _F32), pltpu.VMEM((2 * t, gd), _F32)],
        compiler_params=params, name="fourier_factored",
    )(proj, w_c, w1, twc, tws, w2)


N_LEVELS = int(np.log2(CHUNK))

_NN = (((1,), (0,)), ((), ()))
_NT = (((1,), (1,)), ((), ()))
_TN = (((0,), (0,)), ((), ()))

_C_INCL, _C_STRICT, _C_EYE, _C_LEVEL0 = 0, 1, 2, 3
_C_LEFT = _C_LEVEL0 + N_LEVELS
_C_RIGHT = _C_LEFT + 1


def _scan_consts(reverse):
    L = CHUNK
    t = np.arange(L)[:, None]
    s = np.arange(L)[None, :]
    if reverse:
        t, s = s, t
    strict = (s < t)
    incl = (s <= t)
    levels = []
    bs = 1
    while bs < L:
        levels.append((t // (2 * bs) == s // (2 * bs)) & (t % (2 * bs) >= bs) & (s % (2 * bs) < bs))
        bs *= 2
    pair = lambda m: np.concatenate([m, m], axis=1)
    left = np.concatenate([np.ones((L, HEAD_DIM)), np.zeros((L, HEAD_DIM))], axis=1)
    slabs = [pair(incl), pair(strict), pair(np.eye(L))] + [pair(l) for l in levels] + [left, 1.0 - left]
    return np.stack(slabs).astype(np.float32)


def _bdot(a, b, dims):
    return lax.dot_general(a, b, dims, preferred_element_type=_F32)


def _wkv_kernel(c_ref, r_ref, k_ref, v_ref, rp_ref, kp_ref, vp_ref, rn_ref, kn_ref, vn_ref,
                cwr_ref, cwk_ref, cwv_ref, wdn_ref, adn_ref, wup_ref, aup_ref, w0_ref, a0_ref,
                kkw_ref, kaw_ref, rkw_ref, s0_ref, y_ref, bo_ref, st_ref, s_scr, *, nsub, npp, reverse):
    L, N = CHUNK, HEAD_DIM
    tb = nsub * L
    j = pl.program_id(2)
    nblk = pl.num_programs(2)
    bf = lambda x: x.astype(_BF16)
    inclp = c_ref[_C_INCL]
    left = c_ref[_C_LEFT]
    right = c_ref[_C_RIGHT]
    left_b, right_b = bf(left), bf(right)
    bdmask = jnp.concatenate([left, right], axis=0)
    bdmask_b = bf(bdmask)
    incl_b = bf(inclp[:, :L])
    strict_b = bf(c_ref[_C_STRICT])
    incl2_b = bf(jnp.concatenate([inclp, inclp], axis=1))
    level_b = [(bf(c_ref[_C_LEVEL0 + lv] * left), bf(c_ref[_C_LEVEL0 + lv] * right)) for lv in range(N_LEVELS)]

    tblk = (nblk - 1 - j) if reverse else j
    has_prev = (tblk > 0).astype(_F32)
    has_next = (tblk < nblk - 1).astype(_F32)
    row = lax.broadcasted_iota(jnp.int32, (tb, 1), 0)

    def conv(u_ref, up_ref, un_ref, cw_ref):
        u = u_ref[0]
        before = jnp.where(row == 0, up_ref[0, 7:8, :] * has_prev, pltpu.roll(u, 1, 0))
        after = jnp.where(row == tb - 1, un_ref[0, 0:1, :] * has_next, pltpu.roll(u, tb - 1, 0))
        return before * cw_ref[0:1, :] + u * cw_ref[1:2, :] + after * cw_ref[2:3, :]

    def segsum(x):
        x_h = x.astype(_BF16)
        x_l = (x - x_h.astype(_F32)).astype(_BF16)
        return (jnp.dot(x_h, bdmask_b, preferred_element_type=_F32)
                + jnp.dot(x_l, bdmask_b, preferred_element_type=_F32))

    r_all = conv(r_ref, rp_ref, rn_ref, cwr_ref)
    k_all = conv(k_ref, kp_ref, kn_ref, cwk_ref)
    v_all = conv(v_ref, vp_ref, vn_ref, cwv_ref)
    w_up = jnp.dot(jnp.tanh(wdn_ref[0]).astype(_BF16), wup_ref[...].astype(_BF16), preferred_element_type=_F32)
    a_up = jnp.dot(adn_ref[0].astype(_BF16), aup_ref[...].astype(_BF16), preferred_element_type=_F32)
    z = -(w0_ref[...] + w_up)
    softplus = jnp.maximum(z, 0.0) + jnp.log(1.0 + jnp.exp(-jnp.abs(z)))
    lw_all = -jnp.exp(-softplus - 0.5)
    a_rate = jax.nn.sigmoid(a0_ref[...] + a_up)
    kd_all = k_all * (1.0 + (a_rate - 1.0) * kaw_ref[...])
    kkr = k_all * kkw_ref[...]
    rkd = r_all * kd_all * rkw_ref[...]
    kk_parts, bonus_parts = [], []
    for q in range(npp):
        sl = slice(q * LANES, (q + 1) * LANES)
        kq = kkr[:, sl]
        kk_parts.append(kq * lax.rsqrt(segsum(kq * kq) + 1e-12))
        bonus_parts.append(segsum(rkd[:, sl]) * v_all[:, sl])
    bo_ref[0] = jnp.concatenate(bonus_parts, axis=1)
    kk_all = jnp.concatenate(kk_parts, axis=1)
    bb_all = kk_all * a_rate

    def bd(x):
        return jnp.concatenate([x * left_b, x * right_b], axis=0)

    @pl.when(j == 0)
    def _():
        zero = jnp.zeros((N, N), _F32)
        for q in range(npp):
            top = jnp.concatenate([s0_ref[0, 2 * q], zero], axis=1)
            bot = jnp.concatenate([zero, s0_ref[0, 2 * q + 1]], axis=1)
            s_scr[q] = jnp.concatenate([top, bot], axis=0)

    order = range(nsub - 1, -1, -1) if reverse else range(nsub)
    last_row = 0 if reverse else L - 1

    chains = [(q, i) for i in order for q in range(npp)]

    def each(fn, *lists):
        return [fn(*args) for args in zip(*lists)]

    def tile(x, c):
        q, i = c
        return x[i * L:(i + 1) * L, q * LANES:(q + 1) * LANES]

    lw = [tile(lw_all, c) for c in chains]

    def cumulative(x):
        x_h = x.astype(_BF16)
        rem = x - x_h.astype(_F32)
        x_m = rem.astype(_BF16)
        x_l = (rem - x_m.astype(_F32)).astype(_BF16)
        return (jnp.dot(incl_b, x_h, preferred_element_type=_F32)
                + jnp.dot(incl_b, x_m, preferred_element_type=_F32)
                + jnp.dot(incl_b, x_l, preferred_element_type=_F32))

    cum = each(cumulative, lw)
    tot = [x[last_row:last_row + 1, :] for x in cum]
    g_iv = [jnp.exp(-x) for x in cum]
    g_rem = each(lambda t_, x: jnp.exp(t_ - x), tot, cum)
    v = [bf(tile(v_all, c)) for c in chains]
    rt = [tile(r_all, c) * jnp.exp(x) for c, x in zip(chains, cum)]
    at = [bf(-tile(kk_all, c) * jnp.exp(x - w)) for c, x, w in zip(chains, cum, lw)]
    kd = [tile(kd_all, c) for c in chains]
    bb = [tile(bb_all, c) for c in chains]
    bh = each(lambda x, g: bf(x * g), bb, g_rem)
    kh = each(lambda x, g: bf(x * g), kd, g_rem)
    gram = each(lambda a, r_, b_, k_, g: _bdot(jnp.concatenate([a, bf(r_)], axis=0),
                                               jnp.concatenate([bd(bf(b_ * g)), bd(bf(k_ * g))], axis=0), _NT),
                at, rt, bb, kd, g_iv)
    a_ab = [bf(g[:L, :LANES]) for g in gram]
    a_ak = [bf(g[:L, LANES:]) * strict_b for g in gram]
    a_rbk = [bf(g[L:, :]) * incl2_b for g in gram]
    tinv = [c_ref[_C_EYE] + g[:L, :LANES] * c_ref[_C_LEVEL0] for g in gram]
    for lv in range(1, N_LEVELS):
        m_left, m_right = level_b[lv]
        tinv_b = [bf(t_) for t_ in tinv]
        tx = each(lambda t_, a: _bdot(t_, jnp.concatenate([a * m_left, a * m_right], axis=0), _NN), tinv_b, a_ab)
        tinv = each(lambda t_, tb_, x: t_ + _bdot(bf(x), bd(tb_), _NN), tinv, tinv_b, tx)
    av = each(lambda a, v_: _bdot(a, bd(v_), _NN), a_ak, v)
    tz = each(lambda t_, a, x: bf(_bdot(bf(t_), jnp.concatenate([bd(a), bd(bf(x))], axis=1), _NN)), tinv, at, av)
    wt = [x[:, :LANES] for x in tz]
    u0 = [x[:, LANES:] for x in tz]
    zero_bd = jnp.zeros((LANES, LANES), _BF16)
    qy = each(lambda a, w, u, v_: _bdot(
        a, jnp.concatenate([jnp.concatenate([bd(w), bd(u)], axis=1),
                            jnp.concatenate([zero_bd, bd(v_)], axis=1)], axis=0), _NN),
        a_rbk, wt, u0, v)
    qq = each(lambda r_, x: bf(r_ + x[:, :LANES]), rt, qy)
    y0 = [x[:, LANES:] for x in qy]
    pm = each(lambda w, b_: bf(_bdot(w, b_, _TN)) * bdmask_b, wt, bh)
    cp = each(lambda u, v_, b_, k_: _bdot(jnp.concatenate([u, v_], axis=0), jnp.concatenate([b_, k_], axis=0),
                                          _TN) * bdmask, u0, v, bh, kh)
    gl = [jnp.exp(x) for x in tot]

    for n, (q, i) in enumerate(chains):
        s0 = s_scr[q]
        s0_b = bf(s0)
        y_ref[0, i * L:(i + 1) * L, q * LANES:(q + 1) * LANES] = _bdot(qq[n], s0_b, _NT) + y0[n]
        s_scr[q] = s0 * gl[n] + _bdot(s0_b, pm[n], _NN) + cp[n]

    @pl.when(j == pl.num_programs(2) - 1)
    def _():
        for q in range(npp):
            s = s_scr[q]
            st_ref[0, 2 * q] = s[:N, :N]
            st_ref[0, 2 * q + 1] = s[N:, N:]


def wkv_scan(proj, p, di, s0, reverse):
    b, t, _ = proj.shape
    d = D_RWKV
    nsub = CHUNKS_PER_STEP
    npp = PAIRS_PER_STEP
    tb = nsub * CHUNK
    width = LANES * npp
    nblk = t // tb
    ngrp = d // width
    halo = 8
    consts = jnp.asarray(_scan_consts(reverse))
    pad_rows = lambda w: jnp.zeros((LANES, d), _F32).at[di * w.shape[0]:(di + 1) * w.shape[0]].set(w)
    wup = pad_rows(p["w_decay_up"][di])
    aup = pad_rows(p["a_up"][di])
    row = lambda a: a.reshape(1, d)

    tmap = (lambda j: nblk - 1 - j) if reverse else (lambda j: j)

    def seq(col0):
        return pl.BlockSpec((1, tb, width), lambda i, g, j: (i, tmap(j), col0 // width + g))

    def prev(col0):
        return pl.BlockSpec((1, halo, width),
                            lambda i, g, j: (i, jnp.maximum(tmap(j) * (tb // halo) - 1, 0), col0 // width + g))

    def nxt(col0):
        return pl.BlockSpec((1, halo, width),
                            lambda i, g, j: (i, jnp.minimum((tmap(j) + 1) * (tb // halo), t // halo - 1),
                                             col0 // width + g))

    def lowrank(col0):
        return pl.BlockSpec((1, tb, LANES), lambda i, g, j: (i, tmap(j), col0 // LANES))

    def cols(nrows, col0=0):
        return pl.BlockSpec((nrows, width), lambda i, g, j: (0, col0 // width + g))

    out_seq = pl.BlockSpec((1, tb, width), lambda i, g, j: (i, tmap(j), g))
    st_spec = pl.BlockSpec((1, 2 * npp, HEAD_DIM, HEAD_DIM), lambda i, g, j: (i, g, 0, 0))
    in_specs = ([pl.BlockSpec(consts.shape, lambda i, g, j: (0, 0, 0))]
                + [seq(P_R), seq(P_K), seq(P_V), prev(P_R), prev(P_K), prev(P_V), nxt(P_R), nxt(P_K), nxt(P_V)]
                + [cols(SHORT_CONV, 0), cols(SHORT_CONV, d), cols(SHORT_CONV, 2 * d)]
                + [lowrank(P_WDN), lowrank(P_ADN), cols(LANES), cols(LANES)]
                + [cols(1)] * 5 + [st_spec])
    y, bonus, st = pl.pallas_call(
        functools.partial(_wkv_kernel, nsub=nsub, npp=npp, reverse=reverse),
        grid=(b, ngrp, nblk),
        in_specs=in_specs,
        out_specs=[out_seq, out_seq, st_spec],
        out_shape=[jax.ShapeDtypeStruct((b, t, d), _F32),
                   jax.ShapeDtypeStruct((b, t, d), _F32),
                   jax.ShapeDtypeStruct(s0.shape, _F32)],
        scratch_shapes=[pltpu.VMEM((npp, LANES, LANES), _F32)],
        compiler_params=pltpu.CompilerParams(
            dimension_semantics=("parallel", "parallel", "arbitrary"), vmem_limit_bytes=VMEM_LIMIT),
        name="wkv_scan_bwd" if reverse else "wkv_scan_fwd",
    )(consts, proj, proj, proj, proj, proj, proj, proj, proj, proj,
      p["conv_w"], p["conv_w"], p["conv_w"], proj, proj, wup, aup,
      row(p["w_decay0"][di]), row(p["a0"][di]), row(p["k_k"]), row(p["k_a"]), row(p["r_k"]), s0)
    return y, bonus, st


def _layer_norm_rows(x, g, b):
    mu = jnp.mean(x, axis=-1, keepdims=True)
    xc = x - mu
    var = jnp.mean(xc * xc, axis=-1, keepdims=True)
    return xc * lax.rsqrt(var + LN_EPS) * g + b


def _post_kernel(yf_ref, yb_ref, bf_ref, bb_ref, gdn_ref, gf_ref, gr_ref, fo_ref, x_ref,
                 g1_ref, sc2_ref, sh2_ref, gup_ref, gng_ref, gnb_ref, wfo_ref, wro_ref, wout_ref,
                 l1g_ref, l1b_ref, wr_ref, bd_ref, x1_ref, h2_ref, lg_ref):
    bdm = bd_ref[...]

    def segsum(x):
        x_h = x.astype(_BF16)
        x_l = (x - x_h.astype(_F32)).astype(_BF16)
        return jnp.dot(x_h, bdm, preferred_element_type=_F32) + jnp.dot(x_l, bdm, preferred_element_type=_F32)

    y = yf_ref[0] + yb_ref[0]
    parts = []
    for q in range(D_RWKV // LANES):
        yq = y[:, q * LANES:(q + 1) * LANES]
        dq = yq - segsum(yq) * (1.0 / HEAD_DIM)
        parts.append(dq * lax.rsqrt(segsum(dq * dq) * (1.0 / HEAD_DIM) + GN_EPS))
    yn = jnp.concatenate(parts, axis=1) * gng_ref[...] + gnb_ref[...]
    g = jnp.dot(jax.nn.sigmoid(gdn_ref[0]).astype(_BF16), gup_ref[...], preferred_element_type=_F32)
    z = (yn + bf_ref[0] + bb_ref[0]) * g
    o_r = jnp.dot(z.astype(_BF16), wro_ref[...], preferred_element_type=_F32)
    o_f = jnp.dot(fo_ref[0].astype(_BF16), wfo_ref[...], preferred_element_type=_F32)
    merged = jax.nn.sigmoid(gf_ref[0]) * o_f + jax.nn.sigmoid(gr_ref[0]) * o_r
    mix = jnp.dot(merged.astype(_BF16), wout_ref[...], preferred_element_type=_F32)
    x1 = _layer_norm_rows(DEEPNORM_ALPHA * x_ref[0] + g1_ref[0] * mix, l1g_ref[...], l1b_ref[...])
    x1_ref[0] = x1
    h2 = x1 * (1.0 + sc2_ref[0]) + sh2_ref[0]
    h2_ref[0] = h2.astype(_BF16)
    lg_ref[0] = jnp.dot(h2, wr_ref[...], precision=_HI, preferred_element_type=_F32)


def post_mix(y_f, y_b, bo_f, bo_b, proj, fourier, x, g1, sc2, sh2, p, *, tm=256):
    b, t, d = x.shape
    tm = min(tm, t)
    bdm = jnp.asarray(np.kron(np.eye(2), np.ones((HEAD_DIM, HEAD_DIM))), _BF16)
    tok = lambda w, c0=0: pl.BlockSpec((1, tm, w), lambda i, j: (i, j, c0 // w))
    per_seq = pl.BlockSpec((1, 1, d), lambda i, j: (i, 0, 0))
    full = lambda a: pl.BlockSpec(a.shape, lambda i, j: (0,) * a.ndim)
    row = lambda a: a.reshape(1, -1)
    weights = [p["g_up"].astype(_BF16), row(p["gn_g"]), row(p["gn_b"]), p["w_fo"].astype(_BF16),
               p["w_ro"].astype(_BF16), p["w_out"].astype(_BF16), row(p["ln1_g"]), row(p["ln1_b"]),
               p["w_router"], bdm]
    return pl.pallas_call(
        _post_kernel,
        grid=(b, t // tm),
        in_specs=[tok(d), tok(d), tok(d), tok(d), tok(LORA_G, P_GDN), tok(d, P_GATE_F), tok(d, P_GATE_R),
                  tok(D_FOURIER), tok(d), per_seq, per_seq, per_seq] + [full(w) for w in weights],
        out_specs=[tok(d), tok(d), tok(N_EXPERTS)],
        out_shape=[jax.ShapeDtypeStruct((b, t, d), _F32),
                   jax.ShapeDtypeStruct((b, t, d), _BF16),
                   jax.ShapeDtypeStruct((b, t, N_EXPERTS), _F32)],
        compiler_params=pltpu.CompilerParams(
            dimension_semantics=("parallel", "parallel"), vmem_limit_bytes=VMEM_LIMIT),
        name="post_mix",
    )(y_f, y_b, bo_f, bo_b, proj, proj, proj, fourier, x, g1, sc2, sh2, *weights)


def _expert_kernel(x_ref, g_ref, w1_ref, w3_ref, w2_ref, o_ref):
    f = pl.program_id(2)
    x = x_ref[0].astype(_BF16)
    h1 = jnp.dot(x, w1_ref[0].astype(_BF16), preferred_element_type=_F32)
    h3 = jnp.dot(x, w3_ref[0].astype(_BF16), preferred_element_type=_F32)
    he = (h1 * jax.nn.sigmoid(h1)) * h3
    part = jnp.dot(he.astype(_BF16), w2_ref[0].astype(_BF16), preferred_element_type=_F32)

    @pl.when(f == 0)
    def _():
        o_ref[0] = part

    @pl.when(f > 0)
    def _():
        o_ref[0] += part

    @pl.when(f == pl.num_programs(2) - 1)
    def _():
        o_ref[0] = o_ref[0] * g_ref[0]


def expert_ffn(xe, gate, w1, w3, w2, *, tm=1024, tf=512):
    e, c, d = xe.shape
    f = w1.shape[2]
    tm = min(tm, c)
    return pl.pallas_call(
        _expert_kernel,
        grid=(e, c // tm, f // tf),
        in_specs=[pl.BlockSpec((1, tm, d), lambda i, j, k: (i, j, 0)),
                  pl.BlockSpec((1, tm, 1), lambda i, j, k: (i, j, 0)),
                  pl.BlockSpec((1, d, tf), lambda i, j, k: (i, 0, k)),
                  pl.BlockSpec((1, d, tf), lambda i, j, k: (i, 0, k)),
                  pl.BlockSpec((1, tf, d), lambda i, j, k: (i, k, 0))],
        out_specs=pl.BlockSpec((1, tm, d), lambda i, j, k: (i, j, 0)),
        out_shape=jax.ShapeDtypeStruct((e, c, d), _F32),
        compiler_params=pltpu.CompilerParams(
            dimension_semantics=("parallel", "parallel", "arbitrary"), vmem_limit_bytes=VMEM_LIMIT),
        name="expert_ffn",
    )(xe, gate, w1, w3, w2)


def _layer_norm(x, g, b):
    mu = jnp.mean(x, axis=-1, keepdims=True)
    var = jnp.mean(jnp.square(x - mu), axis=-1, keepdims=True)
    return (x - mu) * lax.rsqrt(var + LN_EPS) * g + b


def _expert_choice_ffn(h2, logits, p):
    b, t, d = h2.shape
    n = b * t
    cap = CAPACITY_FACTOR * n // N_EXPERTS
    hf = h2.reshape(n, d)
    aff = jax.nn.softmax(logits.reshape(n, N_EXPERTS), axis=-1)
    gate, idx = lax.top_k(aff.T, cap)
    xe = hf[idx]
    ye = expert_ffn(xe, gate[..., None], p["w_e1"], p["w_e3"], p["w_e2"])
    out = jnp.zeros((n, d), _F32).at[idx.reshape(-1)].add(ye.reshape(-1, d))
    return out.reshape(b, t, d)


def _trunk_layer(x, cond, s_f0, s_b0, p):
    mod = matmul(jax.nn.silu(cond), p["w_ada"], precise=True) + p["b_ada"]
    sh1, sc1, g1, sh2, sc2, g2 = [a[:, None, :] for a in jnp.split(mod, 6, axis=-1)]
    proj = in_projection(x, sc1, sh1, p["w_in_perm"])
    fourier = fourier_mix(proj)
    y_f, bo_f, s_f = wkv_scan(proj, p, 0, s_f0, False)
    y_b, bo_b, s_b = wkv_scan(proj, p, 1, s_b0, True)
    x1, h2, logits = post_mix(y_f, y_b, bo_f, bo_b, proj, fourier, x, g1, sc2, sh2, p)
    x2 = _layer_norm(DEEPNORM_ALPHA * x1 + g2 * _expert_choice_ffn(h2, logits, p), p["ln2_g"], p["ln2_b"])
    return x2, s_f, s_b


def kernel(x_prompt, x_sample, state_fwd, state_bwd, c, c_ctx, w_ada, b_ada, w_in, conv_w, w_decay0, w_decay_up, a0, a_up, g_up, k_k, k_a, r_k, gn_g, gn_b, w_fo, w_ro, w_out, ln1_g, ln1_b, w_router, w_e1, w_e3, w_e2, ln2_g, ln2_b):
    n_ctx = x_prompt.shape[0]
    cond_ctx = jnp.broadcast_to(c_ctx, (n_ctx, D_MODEL))
    zero_state = jnp.zeros((n_ctx, N_HEADS, HEAD_DIM, HEAD_DIM), _F32)
    names = ("w_ada", "b_ada", "w_in", "conv_w", "w_decay0", "w_decay_up", "a0", "a_up", "g_up", "k_k", "k_a",
             "r_k", "gn_g", "gn_b", "w_fo", "w_ro", "w_out", "ln1_g", "ln1_b", "w_router", "w_e1", "w_e3",
             "w_e2", "ln2_g", "ln2_b")
    stacked = (w_ada, b_ada, w_in, conv_w, w_decay0, w_decay_up, a0, a_up, g_up, k_k, k_a, r_k, gn_g, gn_b,
               w_fo, w_ro, w_out, ln1_g, ln1_b, w_router, w_e1, w_e3, w_e2, ln2_g, ln2_b)
    xp, xs = x_prompt, x_sample
    new_f, new_b = [], []
    for l in range(DEPTH):
        p = {k: a[l] for k, a in zip(names, stacked)}
        p["w_in_perm"] = _permute_in_weight(p["w_in"]).astype(_BF16)
        xp, s_f, s_b = _trunk_layer(xp, cond_ctx, zero_state, zero_state, p)
        new_f.append(s_f)
        new_b.append(s_b)
        xs, _, _ = _trunk_layer(xs, c, state_fwd[:, l], state_bwd[:, l], p)
    return (xp, xs, jnp.stack(new_f, axis=1), jnp.stack(new_b, axis=1))
```

```python
import functools

import numpy as np
import jax
import jax.numpy as jnp
from jax import lax
from jax.experimental import pallas as pl
from jax.experimental.pallas import tpu as pltpu

D_MODEL = 1024
N_FOURIER_GROUPS = 4
FOURIER_GROUP_DIM = 128
D_FOURIER = N_FOURIER_GROUPS * FOURIER_GROUP_DIM
HEAD_DIM = 64
D_RWKV = D_MODEL
N_HEADS = D_RWKV // HEAD_DIM
LORA_W = 64
LORA_A = 64
LORA_G = 128
N_DIR = 2
SHORT_CONV = 3
GN_EPS = 64e-5
N_EXPERTS = 16
CAPACITY_FACTOR = 2
D_EXPERT = 2048
LN_EPS = 1e-5
DEPTH = 1
DEEPNORM_ALPHA = (2.0 * DEPTH) ** 0.25
IN_WIDTHS = (D_FOURIER, 3 * D_RWKV, N_DIR * LORA_W, N_DIR * LORA_A, LORA_G, D_MODEL, D_MODEL)
D_IN = sum(IN_WIDTHS)
SPLIT_POINTS = tuple(sum(IN_WIDTHS[:i + 1]) for i in range(len(IN_WIDTHS) - 1))

LANES = 128
CHUNK = HEAD_DIM
CHUNKS_PER_STEP = 4
PAIRS_PER_STEP = 4
VMEM_LIMIT = 48 * 1024 * 1024

_HI = lax.Precision.HIGHEST
_F32 = jnp.float32
_BF16 = jnp.bfloat16


def _dot(a, b, precise):
    if precise:
        return jnp.dot(a, b, precision=_HI, preferred_element_type=_F32)
    return jnp.dot(a.astype(_BF16), b.astype(_BF16), preferred_element_type=_F32)


def _mm_kernel(a_ref, b_ref, o_ref, *, precise):
    o_ref[...] = _dot(a_ref[...], b_ref[...], precise)


def matmul(a, b, *, precise=False, tm=512, tn=512):
    m, k = a.shape
    n = b.shape[1]
    tm = min(tm, m)
    tn = min(tn, n)
    return pl.pallas_call(
        functools.partial(_mm_kernel, precise=precise),
        grid=(pl.cdiv(m, tm), pl.cdiv(n, tn)),
        in_specs=[pl.BlockSpec((tm, k), lambda i, j: (i, 0)),
                  pl.BlockSpec((k, tn), lambda i, j: (0, j))],
        out_specs=pl.BlockSpec((tm, tn), lambda i, j: (i, j)),
        out_shape=jax.ShapeDtypeStruct((m, n), _F32),
        compiler_params=pltpu.CompilerParams(
            dimension_semantics=("parallel", "parallel"), vmem_limit_bytes=VMEM_LIMIT),
    )(a, b)


P_R, P_K, P_V = 0, D_RWKV, 2 * D_RWKV
P_GATE_F = 3 * D_RWKV
P_GATE_R = P_GATE_F + D_MODEL
P_UF = P_GATE_R + D_MODEL
P_WDN = P_UF + D_FOURIER
P_ADN = P_WDN + N_DIR * LORA_W
P_GDN = P_ADN + N_DIR * LORA_A
D_PROJ = P_GDN + LORA_G + 128
assert D_PROJ % 512 == 0 and N_DIR * LORA_W == LANES and N_DIR * LORA_A == LANES and LORA_G == LANES


def _permute_in_weight(w_in):
    s = (0,) + SPLIT_POINTS + (D_IN,)
    seg = [w_in[:, s[i]:s[i + 1]] for i in range(len(IN_WIDTHS))]
    pad = jnp.zeros((w_in.shape[0], D_PROJ - D_IN), w_in.dtype)
    return jnp.concatenate([seg[1], seg[5], seg[6], seg[0], seg[2], seg[3], seg[4], pad], axis=1)


def _inproj_kernel(x_ref, sc_ref, sh_ref, w_ref, o_ref, h_scr):
    nb, tm, d = x_ref.shape

    @pl.when(pl.program_id(2) == 0)
    def _():
        h = x_ref[...] * (1.0 + sc_ref[...]) + sh_ref[...]
        h_scr[...] = h.reshape(nb * tm, d).astype(_BF16)

    out = jnp.dot(h_scr[...], w_ref[...], preferred_element_type=_F32)
    o_ref[...] = out.reshape(nb, tm, out.shape[-1])


def in_projection(x, sc, sh, w, *, rows=1024, tn=512):
    b, t, d = x.shape
    n = w.shape[1]
    tm = min(rows, t)
    nb = max(1, min(rows // tm, b))
    while b % nb:
        nb -= 1
    assert t % tm == 0
    return pl.pallas_call(
        _inproj_kernel,
        grid=(b // nb, t // tm, n // tn),
        in_specs=[pl.BlockSpec((nb, tm, d), lambda i, j, k: (i, j, 0)),
                  pl.BlockSpec((nb, 1, d), lambda i, j, k: (i, 0, 0)),
                  pl.BlockSpec((nb, 1, d), lambda i, j, k: (i, 0, 0)),
                  pl.BlockSpec((d, tn), lambda i, j, k: (0, k))],
        out_specs=pl.BlockSpec((nb, tm, tn), lambda i, j, k: (i, j, k)),
        out_shape=jax.ShapeDtypeStruct((b, t, n), _F32),
        scratch_shapes=[pltpu.VMEM((nb * tm, d), _BF16)],
        compiler_params=pltpu.CompilerParams(
            dimension_semantics=("parallel", "parallel", "arbitrary"), vmem_limit_bytes=VMEM_LIMIT),
        name="in_projection",
    )(x, sc, sh, w)


FOURIER_UNROLL = 8


def _dft_cos_sin(n):
    idx = np.arange(n)
    ang = 2.0 * np.pi * ((idx[:, None] * idx[None, :]) % n) / n
    return np.cos(ang), np.sin(ang)


def _split(x):
    x_h = x.astype(_BF16)
    return x_h, (x - x_h.astype(_F32)).astype(_BF16)


def _dot3(a, b):
    dg = lambda x, y: jnp.dot(x, y, preferred_element_type=_F32)
    return dg(a[0], b[0]) + dg(a[0], b[1]) + dg(a[1], b[0])


def _fourier_dense_kernel(u_ref, wc_ref, wt_ref, o_ref):
    t, gd = u_ref.shape[1], FOURIER_GROUP_DIM
    z = _dot3(_split(u_ref[0]), _split(wc_ref[...]))
    m = _dot3(_split(wt_ref[...]), _split(z))
    o_ref[0] = m[:t, :gd] - m[t:, gd:]


def _fourier_factored_kernel(u_ref, wc_ref, w1_ref, twc_ref, tws_ref, w2_ref, o_ref, p_scr, q_scr, a_scr,
                             *, n1):
    gd = FOURIER_GROUP_DIM
    z = _dot3(_split(u_ref[0]), _split(wc_ref[...]))
    p_scr[...] = z[:, :gd]
    q_scr[...] = z[:, gd:]
    w1 = _split(w1_ref[...])
    w2 = _split(w2_ref[...])

    def stage1(step, carry):
        t2s = [step * FOURIER_UNROLL + u for u in range(FOURIER_UNROLL)]
        rows = [pl.ds(t2, n1, stride=n1) for t2 in t2s]
        ms = [_dot3(w1, _split(jnp.concatenate([p_scr[r, :], q_scr[r, :]], axis=1))) for r in rows]
        for t2, m in zip(t2s, ms):
            x1r = m[:n1, :gd] - m[n1:, gd:]
            x1i = -(m[:n1, gd:] + m[n1:, :gd])
            c = twc_ref[t2]
            s = tws_ref[t2]
            a_scr[pl.ds(t2, n1, stride=2 * n1), :] = x1r * c + x1i * s
            a_scr[pl.ds(n1 + t2, n1, stride=2 * n1), :] = x1i * c - x1r * s
        return carry

    lax.fori_loop(0, n1 // FOURIER_UNROLL, stage1, 0)

    def stage2(step, carry):
        t1s = [step * FOURIER_UNROLL + u for u in range(FOURIER_UNROLL)]
        outs = [_dot3(w2, _split(a_scr[pl.ds(pl.multiple_of(t1 * 2 * n1, 2 * n1), 2 * n1), :])) for t1 in t1s]
        for t1, o in zip(t1s, outs):
            o_ref[0, pl.ds(t1, n1, stride=n1), :] = o
        return carry

    lax.fori_loop(0, n1 // FOURIER_UNROLL, stage2, 0)


def fourier_mix(proj):
    b, t, _ = proj.shape
    gd = FOURIER_GROUP_DIM
    cc, sc = _dft_cos_sin(gd)
    w_c = jnp.asarray(np.concatenate([cc, sc], axis=1) / np.sqrt(gd), _F32)
    u_spec = pl.BlockSpec((1, t, gd), lambda i, g: (i, 0, P_UF // gd + g))
    o_spec = pl.BlockSpec((1, t, gd), lambda i, g: (i, 0, g))
    full = lambda a: pl.BlockSpec(a.shape, lambda i, g: (0,) * a.ndim)
    params = pltpu.CompilerParams(dimension_semantics=("parallel", "parallel"), vmem_limit_bytes=VMEM_LIMIT)
    out_shape = jax.ShapeDtypeStruct((b, t, D_FOURIER), _F32)
    if t <= 512:
        ct, st = _dft_cos_sin(t)
        w_t = jnp.asarray(np.concatenate([ct, st], axis=0) / np.sqrt(t), _F32)
        return pl.pallas_call(
            _fourier_dense_kernel, grid=(b, N_FOURIER_GROUPS),
            in_specs=[u_spec, full(w_c), full(w_t)], out_specs=o_spec, out_shape=out_shape,
            compiler_params=params, name="fourier_dense",
        )(proj, w_c, w_t)
    n1 = int(round(np.sqrt(t)))
    assert n1 * n1 == t and n1 % FOURIER_UNROLL == 0
    c1, s1 = _dft_cos_sin(n1)
    w1 = jnp.asarray(np.concatenate([c1, s1], axis=0) / np.sqrt(n1), _F32)
    w2 = jnp.asarray(np.concatenate([c1, s1], axis=1) / np.sqrt(n1), _F32)
    tw = 2.0 * np.pi * (np.arange(n1)[:, None] * np.arange(n1)[None, :]) / t
    lanes = np.ones((1, 1, gd))
    twc = jnp.asarray(np.cos(tw)[:, :, None] * lanes, _F32)
    tws = jnp.asarray(np.sin(tw)[:, :, None] * lanes, _F32)
    return pl.pallas_call(
        functools.partial(_fourier_factored_kernel, n1=n1), grid=(b, N_FOURIER_GROUPS),
        in_specs=[u_spec, full(w_c), full(w1), full(twc), full(tws), full(w2)],
        out_specs=o_spec, out_shape=out_shape,
        scratch_shapes=[pltpu.VMEM((t, gd), _F32), pltpu.VMEM((t, gd), _F32), pltpu.VMEM((2 * t, gd), _F32)],
        compiler_params=params, name="fourier_factored",
    )(proj, w_c, w1, twc, tws, w2)


N_LEVELS = int(np.log2(CHUNK))

_NN = (((1,), (0,)), ((), ()))
_NT = (((1,), (1,)), ((), ()))
_TN = (((0,), (0,)), ((), ()))

_C_INCL, _C_STRICT, _C_EYE, _C_LEVEL0 = 0, 1, 2, 3
_C_LEFT = _C_LEVEL0 + N_LEVELS
_C_RIGHT = _C_LEFT + 1


def _scan_consts(reverse):
    L = CHUNK
    t = np.arange(L)[:, None]
    s = np.arange(L)[None, :]
    if reverse:
        t, s = s, t
    strict = (s < t)
    incl = (s <= t)
    levels = []
    bs = 1
    while bs < L:
        levels.append((t // (2 * bs) == s // (2 * bs)) & (t % (2 * bs) >= bs) & (s % (2 * bs) < bs))
        bs *= 2
    pair = lambda m: np.concatenate([m, m], axis=1)
    left = np.concatenate([np.ones((L, HEAD_DIM)), np.zeros((L, HEAD_DIM))], axis=1)
    slabs = [pair(incl), pair(strict), pair(np.eye(L))] + [pair(l) for l in levels] + [left, 1.0 - left]
    return np.stack(slabs).astype(np.float32)


def _bdot(a, b, dims):
    return lax.dot_general(a, b, dims, preferred_element_type=_F32)


_T_R, _T_V, _T_KK, _T_KD, _T_BB, _T_LW, _T_BONUS = range(7)
N_TOKEN_ARRAYS = 7
N_SEQ_VIEWS = 11


def _wkv_kernel(c_ref, *refs, nsub, npp, reverse):
    first = refs[:N_SEQ_VIEWS]
    ahead = refs[N_SEQ_VIEWS:2 * N_SEQ_VIEWS]
    (cwr_ref, cwk_ref, cwv_ref, wup_ref, aup_ref, w0_ref, a0_ref, kkw_ref, kaw_ref, rkw_ref, s0_ref,
     y_ref, bo_ref, st_ref, s_scr, tok_scr) = refs[2 * N_SEQ_VIEWS:]
    L, N = CHUNK, HEAD_DIM
    tb = nsub * L
    j = pl.program_id(2)
    nblk = pl.num_programs(2)
    bf = lambda x: x.astype(_BF16)
    inclp = c_ref[_C_INCL]
    left = c_ref[_C_LEFT]
    right = c_ref[_C_RIGHT]
    left_b, right_b = bf(left), bf(right)
    bdmask = jnp.concatenate([left, right], axis=0)
    bdmask_b = bf(bdmask)
    incl_b = bf(inclp[:, :L])
    strict_b = bf(c_ref[_C_STRICT])
    incl2_b = bf(jnp.concatenate([inclp, inclp], axis=1))
    level_b = [(bf(c_ref[_C_LEVEL0 + lv] * left), bf(c_ref[_C_LEVEL0 + lv] * right)) for lv in range(N_LEVELS)]
    row = lax.broadcasted_iota(jnp.int32, (tb, 1), 0)

    def segsum(x):
        x_h = bf(x)
        x_l = bf(x - x_h.astype(_F32))
        return (jnp.dot(x_h, bdmask_b, preferred_element_type=_F32)
                + jnp.dot(x_l, bdmask_b, preferred_element_type=_F32))

    def token_pieces(views, tblk, dst):
        r_ref, k_ref, v_ref, rp_ref, kp_ref, vp_ref, rn_ref, kn_ref, vn_ref, wdn_ref, adn_ref = views
        has_prev = jnp.where(tblk > 0, 1.0, 0.0).astype(_F32)
        has_next = jnp.where(tblk < nblk - 1, 1.0, 0.0).astype(_F32)
        shared = {}
        pieces = []

        def low_rank_inputs():
            shared["wdn"] = bf(jnp.tanh(wdn_ref[0]))
            shared["adn"] = bf(adn_ref[0])

        pieces.append(low_rank_inputs)
        for q in range(npp):
            sl = slice(q * LANES, (q + 1) * LANES)
            vals = {}

            def conv(name, u_ref, up_ref, un_ref, cw_ref, sl=sl, vals=vals):
                u = u_ref[0, :, sl]
                before = jnp.where(row == 0, up_ref[0, 7:8, sl] * has_prev, pltpu.roll(u, 1, 0))
                after = jnp.where(row == tb - 1, un_ref[0, 0:1, sl] * has_next, pltpu.roll(u, tb - 1, 0))
                vals[name] = before * cw_ref[0:1, sl] + u * cw_ref[1:2, sl] + after * cw_ref[2:3, sl]

            def rates(sl=sl, vals=vals):
                w_up = jnp.dot(shared["wdn"], bf(wup_ref[:, sl]), preferred_element_type=_F32)
                a_up = jnp.dot(shared["adn"], bf(aup_ref[:, sl]), preferred_element_type=_F32)
                z = -(w0_ref[:, sl] + w_up)
                softplus = jnp.maximum(z, 0.0) + jnp.log(1.0 + jnp.exp(-jnp.abs(z)))
                tok_scr[dst, _T_LW, :, sl] = -jnp.exp(-softplus - 0.5)
                vals["a"] = jax.nn.sigmoid(a0_ref[:, sl] + a_up)

            def keys(sl=sl, vals=vals):
                k, a_rate = vals["k"], vals["a"]
                kd = k * (1.0 + (a_rate - 1.0) * kaw_ref[:, sl])
                kkr = k * kkw_ref[:, sl]
                kk = kkr * lax.rsqrt(segsum(kkr * kkr) + 1e-12)
                tok_scr[dst, _T_KD, :, sl] = kd
                tok_scr[dst, _T_KK, :, sl] = kk
                tok_scr[dst, _T_BB, :, sl] = kk * a_rate
                tok_scr[dst, _T_R, :, sl] = vals["r"]
                tok_scr[dst, _T_V, :, sl] = vals["v"]
                tok_scr[dst, _T_BONUS, :, sl] = segsum(vals["r"] * kd * rkw_ref[:, sl]) * vals["v"]

            pieces += [functools.partial(conv, "r", r_ref, rp_ref, rn_ref, cwr_ref),
                       functools.partial(conv, "k", k_ref, kp_ref, kn_ref, cwk_ref),
                       functools.partial(conv, "v", v_ref, vp_ref, vn_ref, cwv_ref),
                       rates, keys]
        return pieces

    def tblk_of(step):
        return (nblk - 1 - step) if reverse else step

    def bd(x):
        return jnp.concatenate([x * left_b, x * right_b], axis=0)

    @pl.when(j == 0)
    def _():
        zero = jnp.zeros((N, N), _F32)
        for q in range(npp):
            top = jnp.concatenate([s0_ref[0, 2 * q], zero], axis=1)
            bot = jnp.concatenate([zero, s0_ref[0, 2 * q + 1]], axis=1)
            s_scr[q] = jnp.concatenate([top, bot], axis=0)
        for piece in token_pieces(first, tblk_of(0), 0):
            piece()

    cur = j % 2
    lookahead = token_pieces(ahead, tblk_of(jnp.minimum(j + 1, nblk - 1)), 1 - cur)

    order = range(nsub - 1, -1, -1) if reverse else range(nsub)
    last_row = 0 if reverse else L - 1
    chains = [(q, i) for i in order for q in range(npp)]

    def each(fn, *lists):
        return [fn(*args) for args in zip(*lists)]

    def tile(which, c):
        q, i = c
        return tok_scr[cur, which, i * L:(i + 1) * L, q * LANES:(q + 1) * LANES]

    def cumulative(x):
        x_h = bf(x)
        rem = x - x_h.astype(_F32)
        x_m = bf(rem)
        x_l = bf(rem - x_m.astype(_F32))
        return (jnp.dot(incl_b, x_h, preferred_element_type=_F32)
                + jnp.dot(incl_b, x_m, preferred_element_type=_F32)
                + jnp.dot(incl_b, x_l, preferred_element_type=_F32))

    e = {}
    stages = []

    def stage(fn):
        stages.append(fn)
        return fn

    @stage
    def _():
        bo_ref[0] = tok_scr[cur, _T_BONUS]
        e["lw"] = [tile(_T_LW, c) for c in chains]
        e["cum"] = each(cumulative, e["lw"])
        e["tot"] = [x[last_row:last_row + 1, :] for x in e["cum"]]

    @stage
    def _():
        cum, tot, lw = e["cum"], e["tot"], e["lw"]
        g_iv = [jnp.exp(-x) for x in cum]
        g_rem = each(lambda t_, x: jnp.exp(t_ - x), tot, cum)
        e["v"] = [bf(tile(_T_V, c)) for c in chains]
        e["rt"] = [tile(_T_R, c) * jnp.exp(x) for c, x in zip(chains, cum)]
        e["at"] = [bf(-tile(_T_KK, c) * jnp.exp(x - w)) for c, x, w in zip(chains, cum, lw)]
        kd = [tile(_T_KD, c) for c in chains]
        bb = [tile(_T_BB, c) for c in chains]
        e["bh"] = each(lambda x, g: bf(x * g), bb, g_rem)
        e["kh"] = each(lambda x, g: bf(x * g), kd, g_rem)
        e["gl"] = [jnp.exp(x) for x in tot]
        e["gram"] = each(lambda a, r_, b_, k_, g: _bdot(
            jnp.concatenate([a, bf(r_)], axis=0),
            jnp.concatenate([bd(bf(b_ * g)), bd(bf(k_ * g))], axis=0), _NT), e["at"], e["rt"], bb, kd, g_iv)

    @stage
    def _():
        gram = e["gram"]
        e["a_ab"] = [bf(g[:L, :LANES]) for g in gram]
        e["a_ak"] = [bf(g[:L, LANES:]) * strict_b for g in gram]
        e["a_rbk"] = [bf(g[L:, :]) * incl2_b for g in gram]
        e["tinv"] = [c_ref[_C_EYE] + g[:L, :LANES] * c_ref[_C_LEVEL0] for g in gram]

    for lv in range(1, N_LEVELS):
        @stage
        def _(lv=lv):
            m_left, m_right = level_b[lv]
            e["tinv_b"] = [bf(t_) for t_ in e["tinv"]]
            e["tx"] = each(lambda t_, a: _bdot(t_, jnp.concatenate([a * m_left, a * m_right], axis=0), _NN),
                           e["tinv_b"], e["a_ab"])

        @stage
        def _():
            e["tinv"] = each(lambda t_, tb_, x: t_ + _bdot(bf(x), bd(tb_), _NN), e["tinv"], e["tinv_b"], e["tx"])

    @stage
    def _():
        e["av"] = each(lambda a, v_: _bdot(a, bd(v_), _NN), e["a_ak"], e["v"])

    @stage
    def _():
        tz = each(lambda t_, a, x: bf(_bdot(bf(t_), jnp.concatenate([bd(a), bd(bf(x))], axis=1), _NN)),
                  e["tinv"], e["at"], e["av"])
        e["wt"] = [x[:, :LANES] for x in tz]
        e["u0"] = [x[:, LANES:] for x in tz]

    @stage
    def _():
        zero_bd = jnp.zeros((LANES, LANES), _BF16)
        qy = each(lambda a, w, u, v_: _bdot(
            a, jnp.concatenate([jnp.concatenate([bd(w), bd(u)], axis=1),
                                jnp.concatenate([zero_bd, bd(v_)], axis=1)], axis=0), _NN),
            e["a_rbk"], e["wt"], e["u0"], e["v"])
        e["qq"] = each(lambda r_, x: bf(r_ + x[:, :LANES]), e["rt"], qy)
        e["y0"] = [x[:, LANES:] for x in qy]

    @stage
    def _():
        e["pm"] = each(lambda w, b_: bf(_bdot(w, b_, _TN)) * bdmask_b, e["wt"], e["bh"])
        e["cp"] = each(lambda u, v_, b_, k_: _bdot(jnp.concatenate([u, v_], axis=0),
                                                   jnp.concatenate([b_, k_], axis=0), _TN) * bdmask,
                       e["u0"], e["v"], e["bh"], e["kh"])

    for n, (q, i) in enumerate(chains):
        @stage
        def _(n=n, q=q, i=i):
            s0 = s_scr[q]
            s0_b = bf(s0)
            y_ref[0, i * L:(i + 1) * L, q * LANES:(q + 1) * LANES] = _bdot(e["qq"][n], s0_b, _NT) + e["y0"][n]
            s_scr[q] = s0 * e["gl"][n] + _bdot(s0_b, e["pm"][n], _NN) + e["cp"][n]

    issued = 0
    for idx, run in enumerate(stages):
        run()
        due = -(-(idx + 1) * len(lookahead) // len(stages))
        while issued < due:
            lookahead[issued]()
            issued += 1

    @pl.when(j == nblk - 1)
    def _():
        for q in range(npp):
            s = s_scr[q]
            st_ref[0, 2 * q] = s[:N, :N]
            st_ref[0, 2 * q + 1] = s[N:, N:]


def wkv_scan(proj, p, di, s0, reverse):
    b, t, _ = proj.shape
    d = D_RWKV
    nsub = CHUNKS_PER_STEP
    npp = PAIRS_PER_STEP
    tb = nsub * CHUNK
    width = LANES * npp
    nblk = t // tb
    ngrp = d // width
    halo = 8
    consts = jnp.asarray(_scan_consts(reverse))
    pad_rows = lambda w: jnp.zeros((LANES, d), _F32).at[di * w.shape[0]:(di + 1) * w.shape[0]].set(w)
    wup = pad_rows(p["w_decay_up"][di])
    aup = pad_rows(p["a_up"][di])
    row = lambda a: a.reshape(1, d)

    def tblk_of(step):
        return (nblk - 1 - step) if reverse else step

    def seq_views(step_of):
        tmap = lambda j: tblk_of(step_of(j))

        def seq(col0):
            return pl.BlockSpec((1, tb, width), lambda i, g, j: (i, tmap(j), col0 // width + g))

        def prev(col0):
            return pl.BlockSpec((1, halo, width),
                                lambda i, g, j: (i, jnp.maximum(tmap(j) * (tb // halo) - 1, 0), col0 // width + g))

        def nxt(col0):
            return pl.BlockSpec((1, halo, width),
                                lambda i, g, j: (i, jnp.minimum((tmap(j) + 1) * (tb // halo), t // halo - 1),
                                                 col0 // width + g))

        def lowrank(col0):
            return pl.BlockSpec((1, tb, LANES), lambda i, g, j: (i, tmap(j), col0 // LANES))

        return [seq(P_R), seq(P_K), seq(P_V), prev(P_R), prev(P_K), prev(P_V), nxt(P_R), nxt(P_K), nxt(P_V),
                lowrank(P_WDN), lowrank(P_ADN)]

    def cols(nrows, col0=0):
        return pl.BlockSpec((nrows, width), lambda i, g, j: (0, col0 // width + g))

    out_seq = pl.BlockSpec((1, tb, width), lambda i, g, j: (i, tblk_of(j), g))
    st_spec = pl.BlockSpec((1, 2 * npp, HEAD_DIM, HEAD_DIM), lambda i, g, j: (i, g, 0, 0))
    first = seq_views(lambda j: 0)
    ahead = seq_views(lambda j: jnp.minimum(j + 1, nblk - 1))
    assert len(first) == N_SEQ_VIEWS
    in_specs = ([pl.BlockSpec(consts.shape, lambda i, g, j: (0, 0, 0))] + first + ahead
                + [cols(SHORT_CONV, 0), cols(SHORT_CONV, d), cols(SHORT_CONV, 2 * d)]
                + [cols(LANES), cols(LANES)] + [cols(1)] * 5 + [st_spec])
    y, bonus, st = pl.pallas_call(
        functools.partial(_wkv_kernel, nsub=nsub, npp=npp, reverse=reverse),
        grid=(b, ngrp, nblk),
        in_specs=in_specs,
        out_specs=[out_seq, out_seq, st_spec],
        out_shape=[jax.ShapeDtypeStruct((b, t, d), _F32),
                   jax.ShapeDtypeStruct((b, t, d), _F32),
                   jax.ShapeDtypeStruct(s0.shape, _F32)],
        scratch_shapes=[pltpu.VMEM((npp, LANES, LANES), _F32),
                        pltpu.VMEM((2, N_TOKEN_ARRAYS, tb, width), _F32)],
        compiler_params=pltpu.CompilerParams(
            dimension_semantics=("parallel", "parallel", "arbitrary"), vmem_limit_bytes=VMEM_LIMIT),
        name="wkv_scan_bwd" if reverse else "wkv_scan_fwd",
    )(consts, *([proj] * (2 * N_SEQ_VIEWS)), p["conv_w"], p["conv_w"], p["conv_w"], wup, aup,
      row(p["w_decay0"][di]), row(p["a0"][di]), row(p["k_k"]), row(p["k_a"]), row(p["r_k"]), s0)
    return y, bonus, st


def _layer_norm_rows(x, g, b):
    mu = jnp.mean(x, axis=-1, keepdims=True)
    xc = x - mu
    var = jnp.mean(xc * xc, axis=-1, keepdims=True)
    return xc * lax.rsqrt(var + LN_EPS) * g + b


def _post_kernel(yf_ref, yb_ref, bf_ref, bb_ref, gdn_ref, gf_ref, gr_ref, fo_ref, x_ref,
                 g1_ref, sc2_ref, sh2_ref, gup_ref, gng_ref, gnb_ref, wfo_ref, wro_ref, wout_ref,
                 l1g_ref, l1b_ref, wr_ref, bd_ref, x1_ref, h2_ref, lg_ref):
    bdm = bd_ref[...]

    def segsum(x):
        x_h = x.astype(_BF16)
        x_l = (x - x_h.astype(_F32)).astype(_BF16)
        return jnp.dot(x_h, bdm, preferred_element_type=_F32) + jnp.dot(x_l, bdm, preferred_element_type=_F32)

    y = yf_ref[0] + yb_ref[0]
    parts = []
    for q in range(D_RWKV // LANES):
        yq = y[:, q * LANES:(q + 1) * LANES]
        dq = yq - segsum(yq) * (1.0 / HEAD_DIM)
        parts.append(dq * lax.rsqrt(segsum(dq * dq) * (1.0 / HEAD_DIM) + GN_EPS))
    yn = jnp.concatenate(parts, axis=1) * gng_ref[...] + gnb_ref[...]
    g = jnp.dot(jax.nn.sigmoid(gdn_ref[0]).astype(_BF16), gup_ref[...], preferred_element_type=_F32)
    z = (yn + bf_ref[0] + bb_ref[0]) * g
    o_r = jnp.dot(z.astype(_BF16), wro_ref[...], preferred_element_type=_F32)
    o_f = jnp.dot(fo_ref[0].astype(_BF16), wfo_ref[...], preferred_element_type=_F32)
    merged = jax.nn.sigmoid(gf_ref[0]) * o_f + jax.nn.sigmoid(gr_ref[0]) * o_r
    mix = jnp.dot(merged.astype(_BF16), wout_ref[...], preferred_element_type=_F32)
    x1 = _layer_norm_rows(DEEPNORM_ALPHA * x_ref[0] + g1_ref[0] * mix, l1g_ref[...], l1b_ref[...])
    x1_ref[0] = x1
    h2 = x1 * (1.0 + sc2_ref[0]) + sh2_ref[0]
    h2_ref[0] = h2.astype(_BF16)
    lg_ref[0] = jnp.dot(h2, wr_ref[...], precision=_HI, preferred_element_type=_F32)


def post_mix(y_f, y_b, bo_f, bo_b, proj, fourier, x, g1, sc2, sh2, p, *, tm=256):
    b, t, d = x.shape
    tm = min(tm, t)
    bdm = jnp.asarray(np.kron(np.eye(2), np.ones((HEAD_DIM, HEAD_DIM))), _BF16)
    tok = lambda w, c0=0: pl.BlockSpec((1, tm, w), lambda i, j: (i, j, c0 // w))
    per_seq = pl.BlockSpec((1, 1, d), lambda i, j: (i, 0, 0))
    full = lambda a: pl.BlockSpec(a.shape, lambda i, j: (0,) * a.ndim)
    row = lambda a: a.reshape(1, -1)
    weights = [p["g_up"].astype(_BF16), row(p["gn_g"]), row(p["gn_b"]), p["w_fo"].astype(_BF16),
               p["w_ro"].astype(_BF16), p["w_out"].astype(_BF16), row(p["ln1_g"]), row(p["ln1_b"]),
               p["w_router"], bdm]
    return pl.pallas_call(
        _post_kernel,
        grid=(b, t // tm),
        in_specs=[tok(d), tok(d), tok(d), tok(d), tok(LORA_G, P_GDN), tok(d, P_GATE_F), tok(d, P_GATE_R),
                  tok(D_FOURIER), tok(d), per_seq, per_seq, per_seq] + [full(w) for w in weights],
        out_specs=[tok(d), tok(d), tok(N_EXPERTS)],
        out_shape=[jax.ShapeDtypeStruct((b, t, d), _F32),
                   jax.ShapeDtypeStruct((b, t, d), _BF16),
                   jax.ShapeDtypeStruct((b, t, N_EXPERTS), _F32)],
        compiler_params=pltpu.CompilerParams(
            dimension_semantics=("parallel", "parallel"), vmem_limit_bytes=VMEM_LIMIT),
        name="post_mix",
    )(y_f, y_b, bo_f, bo_b, proj, proj, proj, fourier, x, g1, sc2, sh2, *weights)


def _expert_kernel(x_ref, g_ref, w1_ref, w3_ref, w2_ref, o_ref):
    f = pl.program_id(2)
    x = x_ref[0].astype(_BF16)
    h1 = jnp.dot(x, w1_ref[0].astype(_BF16), preferred_element_type=_F32)
    h3 = jnp.dot(x, w3_ref[0].astype(_BF16), preferred_element_type=_F32)
    he = (h1 * jax.nn.sigmoid(h1)) * h3
    part = jnp.dot(he.astype(_BF16), w2_ref[0].astype(_BF16), preferred_element_type=_F32)

    @pl.when(f == 0)
    def _():
        o_ref[0] = part

    @pl.when(f > 0)
    def _():
        o_ref[0] += part

    @pl.when(f == pl.num_programs(2) - 1)
    def _():
        o_ref[0] = o_ref[0] * g_ref[0]


def expert_ffn(xe, gate, w1, w3, w2, *, tm=1024, tf=512):
    e, c, d = xe.shape
    f = w1.shape[2]
    tm = min(tm, c)
    return pl.pallas_call(
        _expert_kernel,
        grid=(e, c // tm, f // tf),
        in_specs=[pl.BlockSpec((1, tm, d), lambda i, j, k: (i, j, 0)),
                  pl.BlockSpec((1, tm, 1), lambda i, j, k: (i, j, 0)),
                  pl.BlockSpec((1, d, tf), lambda i, j, k: (i, 0, k)),
                  pl.BlockSpec((1, d, tf), lambda i, j, k: (i, 0, k)),
                  pl.BlockSpec((1, tf, d), lambda i, j, k: (i, k, 0))],
        out_specs=pl.BlockSpec((1, tm, d), lambda i, j, k: (i, j, 0)),
        out_shape=jax.ShapeDtypeStruct((e, c, d), _F32),
        compiler_params=pltpu.CompilerParams(
            dimension_semantics=("parallel", "parallel", "arbitrary"), vmem_limit_bytes=VMEM_LIMIT),
        name="expert_ffn",
    )(xe, gate, w1, w3, w2)


COMBINE_WINDOW = 256
COMBINE_TOKENS = 512


def _combine_kernel(starts_ref, idx_ref, ye_hbm, x1_ref, g2_ref, lg_ref, lb_ref, o_ref, buf, acc, sem,
                    *, tm, cap, win):
    j = pl.program_id(0)
    n_exp = idx_ref.shape[0]
    tok = j * tm + lax.broadcasted_iota(jnp.int32, (tm, 1), 0)

    def first_row(e):
        return (starts_ref[e, j] // LANES) * LANES

    def window_start(e, k):
        return pl.multiple_of(jnp.minimum(first_row(e) + k * win, cap - win), LANES)

    def n_windows(e):
        return jnp.maximum((starts_ref[e, j + 1] - first_row(e) + win - 1) // win, 1)

    def copy(e, k, slot):
        return pltpu.make_async_copy(ye_hbm.at[e, pl.ds(window_start(e, k), win), :], buf.at[slot], sem.at[slot])

    def place(e, k, slot):
        ids = idx_ref[e:e + 1, pl.ds(window_start(e, k), win)]
        row_of = window_start(e, k) + lax.broadcasted_iota(jnp.int32, (1, win), 1)
        ids = jnp.where(row_of >= first_row(e) + k * win, ids, -1)
        onehot = (tok == ids).astype(_BF16)
        acc[...] += jnp.dot(jnp.concatenate([onehot, onehot], axis=1), jnp.concatenate(_split(buf[slot]), axis=0),
                            preferred_element_type=_F32)

    acc[...] = jnp.zeros_like(acc)
    copy(0, 0, 0).start()
    for e in range(n_exp):
        if e + 1 < n_exp:
            copy(e + 1, 0, (e + 1) % 2).start()
        copy(e, 0, e % 2).wait()
        place(e, 0, e % 2)

        def further(k, carry, e=e):
            cp = copy(e, k, 2)
            cp.start()
            cp.wait()
            place(e, k, 2)
            return carry

        lax.fori_loop(1, n_windows(e), further, 0)

    x = DEEPNORM_ALPHA * x1_ref[...] + g2_ref[0] * acc[...]
    o_ref[...] = _layer_norm_rows(x, lg_ref[...], lb_ref[...])


def combine(ye, idx_s, x1, g2, ln_g, ln_b):
    n_exp, cap, d = ye.shape
    b, t, _ = x1.shape
    n = b * t
    tm = min(COMBINE_TOKENS, t)
    win = min(COMBINE_WINDOW, cap)
    ntiles = n // tm
    bounds = jnp.arange(ntiles + 1, dtype=jnp.int32) * tm
    starts = jnp.sum(idx_s[:, :, None] < bounds[None, None, :], axis=1).astype(jnp.int32)
    row = lambda a: a.reshape(1, d)
    grid_spec = pltpu.PrefetchScalarGridSpec(
        num_scalar_prefetch=1,
        grid=(ntiles,),
        in_specs=[pl.BlockSpec((n_exp, cap), lambda j, st: (0, 0)),
                  pl.BlockSpec(memory_space=pl.ANY),
                  pl.BlockSpec((tm, d), lambda j, st: (j, 0)),
                  pl.BlockSpec((1, 1, d), lambda j, st: (j * tm // t, 0, 0)),
                  pl.BlockSpec((1, d), lambda j, st: (0, 0)),
                  pl.BlockSpec((1, d), lambda j, st: (0, 0))],
        out_specs=pl.BlockSpec((tm, d), lambda j, st: (j, 0)),
        scratch_shapes=[pltpu.VMEM((3, win, d), _F32), pltpu.VMEM((tm, d), _F32), pltpu.SemaphoreType.DMA((3,))],
    )
    out = pl.pallas_call(
        functools.partial(_combine_kernel, tm=tm, cap=cap, win=win),
        grid_spec=grid_spec,
        out_shape=jax.ShapeDtypeStruct((n, d), _F32),
        compiler_params=pltpu.CompilerParams(dimension_semantics=("arbitrary",), vmem_limit_bytes=VMEM_LIMIT),
        name="combine",
    )(starts, idx_s, ye, x1.reshape(n, d), g2, row(ln_g), row(ln_b))
    return out.reshape(b, t, d)


def _expert_choice_ffn(h2, logits, p):
    b, t, d = h2.shape
    n = b * t
    cap = CAPACITY_FACTOR * n // N_EXPERTS
    hf = h2.reshape(n, d)
    aff = jax.nn.softmax(logits.reshape(n, N_EXPERTS), axis=-1)
    gate, idx = lax.top_k(aff.T, cap)
    idx_s, gate_s = lax.sort_key_val(idx, gate, dimension=1)
    ye = expert_ffn(hf[idx_s], gate_s[..., None], p["w_e1"], p["w_e3"], p["w_e2"])
    return ye, idx_s


def _trunk_layer(x, cond, s_f0, s_b0, p):
    mod = matmul(jax.nn.silu(cond), p["w_ada"], precise=True) + p["b_ada"]
    sh1, sc1, g1, sh2, sc2, g2 = [a[:, None, :] for a in jnp.split(mod, 6, axis=-1)]
    proj = in_projection(x, sc1, sh1, p["w_in_perm"])
    fourier = fourier_mix(proj)
    y_f, bo_f, s_f = wkv_scan(proj, p, 0, s_f0, False)
    y_b, bo_b, s_b = wkv_scan(proj, p, 1, s_b0, True)
    x1, h2, logits = post_mix(y_f, y_b, bo_f, bo_b, proj, fourier, x, g1, sc2, sh2, p)
    ye, idx_s = _expert_choice_ffn(h2, logits, p)
    x2 = combine(ye, idx_s, x1, g2, p["ln2_g"], p["ln2_b"])
    return x2, s_f, s_b


def kernel(x_prompt, x_sample, state_fwd, state_bwd, c, c_ctx, w_ada, b_ada, w_in, conv_w, w_decay0, w_decay_up, a0, a_up, g_up, k_k, k_a, r_k, gn_g, gn_b, w_fo, w_ro, w_out, ln1_g, ln1_b, w_router, w_e1, w_e3, w_e2, ln2_g, ln2_b):
    n_ctx = x_prompt.shape[0]
    cond_ctx = jnp.broadcast_to(c_ctx, (n_ctx, D_MODEL))
    zero_state = jnp.zeros((n_ctx, N_HEADS, HEAD_DIM, HEAD_DIM), _F32)
    names = ("w_ada", "b_ada", "w_in", "conv_w", "w_decay0", "w_decay_up", "a0", "a_up", "g_up", "k_k", "k_a",
             "r_k", "gn_g", "gn_b", "w_fo", "w_ro", "w_out", "ln1_g", "ln1_b", "w_router", "w_e1", "w_e3",
             "w_e2", "ln2_g", "ln2_b")
    stacked = (w_ada, b_ada, w_in, conv_w, w_decay0, w_decay_up, a0, a_up, g_up, k_k, k_a, r_k, gn_g, gn_b,
               w_fo, w_ro, w_out, ln1_g, ln1_b, w_router, w_e1, w_e3, w_e2, ln2_g, ln2_b)
    xp, xs = x_prompt, x_sample
    new_f, new_b = [], []
    for l in range(DEPTH):
        p = {k: a[l] for k, a in zip(names, stacked)}
        p["w_in_perm"] = _permute_in_weight(p["w_in"]).astype(_BF16)
        xp, s_f, s_b = _trunk_layer(xp, cond_ctx, zero_state, zero_state, p)
        new_f.append(s_f)
        new_b.append(s_b)
        xs, _, _ = _trunk_layer(xs, c, state_fwd[:, l], state_bwd[:, l], p)
    return (xp, xs, jnp.stack(new_f, axis=1), jnp.stack(new_b, axis=1))
```

```python
import functools

import numpy as np
import jax
import jax.numpy as jnp
from jax import lax
from jax.experimental import pallas as pl
from jax.experimental.pallas import tpu as pltpu

D_MODEL = 1024
N_FOURIER_GROUPS = 4
FOURIER_GROUP_DIM = 128
D_FOURIER = N_FOURIER_GROUPS * FOURIER_GROUP_DIM
HEAD_DIM = 64
D_RWKV = D_MODEL
N_HEADS = D_RWKV // HEAD_DIM
LORA_W = 64
LORA_A = 64
LORA_G = 128
N_DIR = 2
SHORT_CONV = 3
GN_EPS = 64e-5
N_EXPERTS = 16
CAPACITY_FACTOR = 2
D_EXPERT = 2048
LN_EPS = 1e-5
DEPTH = 1
DEEPNORM_ALPHA = (2.0 * DEPTH) ** 0.25
IN_WIDTHS = (D_FOURIER, 3 * D_RWKV, N_DIR * LORA_W, N_DIR * LORA_A, LORA_G, D_MODEL, D_MODEL)
D_IN = sum(IN_WIDTHS)
SPLIT_POINTS = tuple(sum(IN_WIDTHS[:i + 1]) for i in range(len(IN_WIDTHS) - 1))

LANES = 128
CHUNK = HEAD_DIM
CHUNKS_PER_STEP = 4
PAIRS_PER_STEP = 4
VMEM_LIMIT = 48 * 1024 * 1024

_HI = lax.Precision.HIGHEST
_F32 = jnp.float32
_BF16 = jnp.bfloat16


def _dot(a, b, precise):
    if precise:
        return jnp.dot(a, b, precision=_HI, preferred_element_type=_F32)
    return jnp.dot(a.astype(_BF16), b.astype(_BF16), preferred_element_type=_F32)


def _mm_kernel(a_ref, b_ref, o_ref, *, precise):
    o_ref[...] = _dot(a_ref[...], b_ref[...], precise)


def matmul(a, b, *, precise=False, tm=512, tn=512):
    m, k = a.shape
    n = b.shape[1]
    tm = min(tm, m)
    tn = min(tn, n)
    return pl.pallas_call(
        functools.partial(_mm_kernel, precise=precise),
        grid=(pl.cdiv(m, tm), pl.cdiv(n, tn)),
        in_specs=[pl.BlockSpec((tm, k), lambda i, j: (i, 0)),
                  pl.BlockSpec((k, tn), lambda i, j: (0, j))],
        out_specs=pl.BlockSpec((tm, tn), lambda i, j: (i, j)),
        out_shape=jax.ShapeDtypeStruct((m, n), _F32),
        compiler_params=pltpu.CompilerParams(
            dimension_semantics=("parallel", "parallel"), vmem_limit_bytes=VMEM_LIMIT),
    )(a, b)


P_R, P_K, P_V = 0, D_RWKV, 2 * D_RWKV
P_GATE_F = 3 * D_RWKV
P_GATE_R = P_GATE_F + D_MODEL
P_UF = P_GATE_R + D_MODEL
P_WDN = P_UF + D_FOURIER
P_ADN = P_WDN + N_DIR * LORA_W
P_GDN = P_ADN + N_DIR * LORA_A
D_PROJ = P_GDN + LORA_G + 128
assert D_PROJ % 512 == 0 and N_DIR * LORA_W == LANES and N_DIR * LORA_A == LANES and LORA_G == LANES


def _permute_in_weight(w_in):
    s = (0,) + SPLIT_POINTS + (D_IN,)
    seg = [w_in[:, s[i]:s[i + 1]] for i in range(len(IN_WIDTHS))]
    pad = jnp.zeros((w_in.shape[0], D_PROJ - D_IN), w_in.dtype)
    return jnp.concatenate([seg[1], seg[5], seg[6], seg[0], seg[2], seg[3], seg[4], pad], axis=1)


def _inproj_kernel(x_ref, sc_ref, sh_ref, w_ref, o_ref, h_scr):
    nb, tm, d = x_ref.shape

    @pl.when(pl.program_id(2) == 0)
    def _():
        h = x_ref[...] * (1.0 + sc_ref[...]) + sh_ref[...]
        h_scr[...] = h.reshape(nb * tm, d).astype(_BF16)

    out = jnp.dot(h_scr[...], w_ref[...], preferred_element_type=_F32)
    o_ref[...] = out.reshape(nb, tm, out.shape[-1])


def in_projection(x, sc, sh, w, *, rows=1024, tn=512):
    b, t, d = x.shape
    n = w.shape[1]
    tm = min(rows, t)
    nb = max(1, min(rows // tm, b))
    while b % nb:
        nb -= 1
    assert t % tm == 0
    return pl.pallas_call(
        _inproj_kernel,
        grid=(b // nb, t // tm, n // tn),
        in_specs=[pl.BlockSpec((nb, tm, d), lambda i, j, k: (i, j, 0)),
                  pl.BlockSpec((nb, 1, d), lambda i, j, k: (i, 0, 0)),
                  pl.BlockSpec((nb, 1, d), lambda i, j, k: (i, 0, 0)),
                  pl.BlockSpec((d, tn), lambda i, j, k: (0, k))],
        out_specs=pl.BlockSpec((nb, tm, tn), lambda i, j, k: (i, j, k)),
        out_shape=jax.ShapeDtypeStruct((b, t, n), _F32),
        scratch_shapes=[pltpu.VMEM((nb * tm, d), _BF16)],
        compiler_params=pltpu.CompilerParams(
            dimension_semantics=("parallel", "parallel", "arbitrary"), vmem_limit_bytes=VMEM_LIMIT),
        name="in_projection",
    )(x, sc, sh, w)


FOURIER_UNROLL = 8


def _dft_cos_sin(n):
    idx = np.arange(n)
    ang = 2.0 * np.pi * ((idx[:, None] * idx[None, :]) % n) / n
    return np.cos(ang), np.sin(ang)


def _split(x):
    x_h = x.astype(_BF16)
    return x_h, (x - x_h.astype(_F32)).astype(_BF16)


def _dot3(a, b):
    dg = lambda x, y: jnp.dot(x, y, preferred_element_type=_F32)
    return dg(a[0], b[0]) + dg(a[0], b[1]) + dg(a[1], b[0])


def _fourier_dense_kernel(u_ref, wc_ref, wt_ref, o_ref):
    t, gd = u_ref.shape[1], FOURIER_GROUP_DIM
    z = _dot3(_split(u_ref[0]), _split(wc_ref[...]))
    m = _dot3(_split(wt_ref[...]), _split(z))
    o_ref[0] = m[:t, :gd] - m[t:, gd:]


def _fourier_factored_kernel(u_ref, wc_ref, w1_ref, twc_ref, tws_ref, w2_ref, o_ref, p_scr, q_scr, a_scr,
                             *, n1):
    gd = FOURIER_GROUP_DIM
    z = _dot3(_split(u_ref[0]), _split(wc_ref[...]))
    p_scr[...] = z[:, :gd]
    q_scr[...] = z[:, gd:]
    w1 = _split(w1_ref[...])
    w2 = _split(w2_ref[...])

    def stage1(step, carry):
        t2s = [step * FOURIER_UNROLL + u for u in range(FOURIER_UNROLL)]
        rows = [pl.ds(t2, n1, stride=n1) for t2 in t2s]
        ms = [_dot3(w1, _split(jnp.concatenate([p_scr[r, :], q_scr[r, :]], axis=1))) for r in rows]
        for t2, m in zip(t2s, ms):
            x1r = m[:n1, :gd] - m[n1:, gd:]
            x1i = -(m[:n1, gd:] + m[n1:, :gd])
            c = twc_ref[t2]
            s = tws_ref[t2]
            a_scr[pl.ds(t2, n1, stride=2 * n1), :] = x1r * c + x1i * s
            a_scr[pl.ds(n1 + t2, n1, stride=2 * n1), :] = x1i * c - x1r * s
        return carry

    lax.fori_loop(0, n1 // FOURIER_UNROLL, stage1, 0)

    def stage2(step, carry):
        t1s = [step * FOURIER_UNROLL + u for u in range(FOURIER_UNROLL)]
        outs = [_dot3(w2, _split(a_scr[pl.ds(pl.multiple_of(t1 * 2 * n1, 2 * n1), 2 * n1), :])) for t1 in t1s]
        for t1, o in zip(t1s, outs):
            o_ref[0, pl.ds(t1, n1, stride=n1), :] = o
        return carry

    lax.fori_loop(0, n1 // FOURIER_UNROLL, stage2, 0)


def fourier_mix(proj):
    b, t, _ = proj.shape
    gd = FOURIER_GROUP_DIM
    cc, sc = _dft_cos_sin(gd)
    w_c = jnp.asarray(np.concatenate([cc, sc], axis=1) / np.sqrt(gd), _F32)
    u_spec = pl.BlockSpec((1, t, gd), lambda i, g: (i, 0, P_UF // gd + g))
    o_spec = pl.BlockSpec((1, t, gd), lambda i, g: (i, 0, g))
    full = lambda a: pl.BlockSpec(a.shape, lambda i, g: (0,) * a.ndim)
    params = pltpu.CompilerParams(dimension_semantics=("parallel", "parallel"), vmem_limit_bytes=VMEM_LIMIT)
    out_shape = jax.ShapeDtypeStruct((b, t, D_FOURIER), _F32)
    if t <= 512:
        ct, st = _dft_cos_sin(t)
        w_t = jnp.asarray(np.concatenate([ct, st], axis=0) / np.sqrt(t), _F32)
        return pl.pallas_call(
            _fourier_dense_kernel, grid=(b, N_FOURIER_GROUPS),
            in_specs=[u_spec, full(w_c), full(w_t)], out_specs=o_spec, out_shape=out_shape,
            compiler_params=params, name="fourier_dense",
        )(proj, w_c, w_t)
    n1 = int(round(np.sqrt(t)))
    assert n1 * n1 == t and n1 % FOURIER_UNROLL == 0
    c1, s1 = _dft_cos_sin(n1)
    w1 = jnp.asarray(np.concatenate([c1, s1], axis=0) / np.sqrt(n1), _F32)
    w2 = jnp.asarray(np.concatenate([c1, s1], axis=1) / np.sqrt(n1), _F32)
    tw = 2.0 * np.pi * (np.arange(n1)[:, None] * np.arange(n1)[None, :]) / t
    lanes = np.ones((1, 1, gd))
    twc = jnp.asarray(np.cos(tw)[:, :, None] * lanes, _F32)
    tws = jnp.asarray(np.sin(tw)[:, :, None] * lanes, _F32)
    return pl.pallas_call(
        functools.partial(_fourier_factored_kernel, n1=n1), grid=(b, N_FOURIER_GROUPS),
        in_specs=[u_spec, full(w_c), full(w1), full(twc), full(tws), full(w2)],
        out_specs=o_spec, out_shape=out_shape,
        scratch_shapes=[pltpu.VMEM((t, gd), _F32), pltpu.VMEM((t, gd), _F32), pltpu.VMEM((2 * t, gd), _F32)],
        compiler_params=params, name="fourier_factored",
    )(proj, w_c, w1, twc, tws, w2)


N_LEVELS = int(np.log2(CHUNK))

_NN = (((1,), (0,)), ((), ()))
_NT = (((1,), (1,)), ((), ()))
_TN = (((0,), (0,)), ((), ()))

_C_INCL, _C_STRICT, _C_EYE, _C_LEVEL0 = 0, 1, 2, 3
_C_LEFT = _C_LEVEL0 + N_LEVELS
_C_RIGHT = _C_LEFT + 1


def _scan_consts(reverse):
    L = CHUNK
    t = np.arange(L)[:, None]
    s = np.arange(L)[None, :]
    if reverse:
        t, s = s, t
    strict = (s < t)
    incl = (s <= t)
    levels = []
    bs = 1
    while bs < L:
        levels.append((t // (2 * bs) == s // (2 * bs)) & (t % (2 * bs) >= bs) & (s % (2 * bs) < bs))
        bs *= 2
    pair = lambda m: np.concatenate([m, m], axis=1)
    left = np.concatenate([np.ones((L, HEAD_DIM)), np.zeros((L, HEAD_DIM))], axis=1)
    slabs = [pair(incl), pair(strict), pair(np.eye(L))] + [pair(l) for l in levels] + [left, 1.0 - left]
    return np.stack(slabs).astype(np.float32)


def _bdot(a, b, dims):
    return lax.dot_general(a, b, dims, preferred_element_type=_F32)


def _wkv_kernel(c_ref, r_ref, k_ref, v_ref, rp_ref, kp_ref, vp_ref, rn_ref, kn_ref, vn_ref,
                cwr_ref, cwk_ref, cwv_ref, wdn_ref, adn_ref, wup_ref, aup_ref, w0_ref, a0_ref,
                kkw_ref, kaw_ref, rkw_ref, s0_ref, y_ref, bo_ref, st_ref, s_scr, *, nsub, npp, reverse):
    L, N = CHUNK, HEAD_DIM
    tb = nsub * L
    j = pl.program_id(2)
    nblk = pl.num_programs(2)
    bf = lambda x: x.astype(_BF16)
    inclp = c_ref[_C_INCL]
    left = c_ref[_C_LEFT]
    right = c_ref[_C_RIGHT]
    left_b, right_b = bf(left), bf(right)
    bdmask = jnp.concatenate([left, right], axis=0)
    bdmask_b = bf(bdmask)
    incl_b = bf(inclp[:, :L])
    strict_b = bf(c_ref[_C_STRICT])
    incl2_b = bf(jnp.concatenate([inclp, inclp], axis=1))
    level_b = [(bf(c_ref[_C_LEVEL0 + lv] * left), bf(c_ref[_C_LEVEL0 + lv] * right)) for lv in range(N_LEVELS)]

    tblk = (nblk - 1 - j) if reverse else j
    has_prev = (tblk > 0).astype(_F32)
    has_next = (tblk < nblk - 1).astype(_F32)
    row = lax.broadcasted_iota(jnp.int32, (tb, 1), 0)

    def conv(u_ref, up_ref, un_ref, cw_ref):
        u = u_ref[0]
        before = jnp.where(row == 0, up_ref[0, 7:8, :] * has_prev, pltpu.roll(u, 1, 0))
        after = jnp.where(row == tb - 1, un_ref[0, 0:1, :] * has_next, pltpu.roll(u, tb - 1, 0))
        return before * cw_ref[0:1, :] + u * cw_ref[1:2, :] + after * cw_ref[2:3, :]

    def segsum(x):
        x_h = x.astype(_BF16)
        x_l = (x - x_h.astype(_F32)).astype(_BF16)
        return (jnp.dot(x_h, bdmask_b, preferred_element_type=_F32)
                + jnp.dot(x_l, bdmask_b, preferred_element_type=_F32))

    r_all = conv(r_ref, rp_ref, rn_ref, cwr_ref)
    k_all = conv(k_ref, kp_ref, kn_ref, cwk_ref)
    v_all = conv(v_ref, vp_ref, vn_ref, cwv_ref)
    w_up = jnp.dot(jnp.tanh(wdn_ref[0]).astype(_BF16), wup_ref[...].astype(_BF16), preferred_element_type=_F32)
    a_up = jnp.dot(adn_ref[0].astype(_BF16), aup_ref[...].astype(_BF16), preferred_element_type=_F32)
    z = -(w0_ref[...] + w_up)
    softplus = jnp.maximum(z, 0.0) + jnp.log(1.0 + jnp.exp(-jnp.abs(z)))
    lw_all = -jnp.exp(-softplus - 0.5)
    a_rate = jax.nn.sigmoid(a0_ref[...] + a_up)
    kd_all = k_all * (1.0 + (a_rate - 1.0) * kaw_ref[...])
    kkr = k_all * kkw_ref[...]
    rkd = r_all * kd_all * rkw_ref[...]
    kk_parts, bonus_parts = [], []
    for q in range(npp):
        sl = slice(q * LANES, (q + 1) * LANES)
        kq = kkr[:, sl]
        kk_parts.append(kq * lax.rsqrt(segsum(kq * kq) + 1e-12))
        bonus_parts.append(segsum(rkd[:, sl]) * v_all[:, sl])
    bo_ref[0] = jnp.concatenate(bonus_parts, axis=1)
    kk_all = jnp.concatenate(kk_parts, axis=1)
    bb_all = kk_all * a_rate

    def bd(x):
        return jnp.concatenate([x * left_b, x * right_b], axis=0)

    @pl.when(j == 0)
    def _():
        zero = jnp.zeros((N, N), _F32)
        for q in range(npp):
            top = jnp.concatenate([s0_ref[0, 2 * q], zero], axis=1)
            bot = jnp.concatenate([zero, s0_ref[0, 2 * q + 1]], axis=1)
            s_scr[q] = jnp.concatenate([top, bot], axis=0)

    order = range(nsub - 1, -1, -1) if reverse else range(nsub)
    last_row = 0 if reverse else L - 1

    chains = [(q, i) for i in order for q in range(npp)]

    def each(fn, *lists):
        return [fn(*args) for args in zip(*lists)]

    def tile(x, c):
        q, i = c
        return x[i * L:(i + 1) * L, q * LANES:(q + 1) * LANES]

    lw = [tile(lw_all, c) for c in chains]

    def cumulative(x):
        x_h = x.astype(_BF16)
        rem = x - x_h.astype(_F32)
        x_m = rem.astype(_BF16)
        x_l = (rem - x_m.astype(_F32)).astype(_BF16)
        return (jnp.dot(incl_b, x_h, preferred_element_type=_F32)
                + jnp.dot(incl_b, x_m, preferred_element_type=_F32)
                + jnp.dot(incl_b, x_l, preferred_element_type=_F32))

    cum = each(cumulative, lw)
    tot = [x[last_row:last_row + 1, :] for x in cum]
    g_iv = [jnp.exp(-x) for x in cum]
    g_rem = each(lambda t_, x: jnp.exp(t_ - x), tot, cum)
    v = [bf(tile(v_all, c)) for c in chains]
    rt = [tile(r_all, c) * jnp.exp(x) for c, x in zip(chains, cum)]
    at = [bf(-tile(kk_all, c) * jnp.exp(x - w)) for c, x, w in zip(chains, cum, lw)]
    kd = [tile(kd_all, c) for c in chains]
    bb = [tile(bb_all, c) for c in chains]
    bh = each(lambda x, g: bf(x * g), bb, g_rem)
    kh = each(lambda x, g: bf(x * g), kd, g_rem)
    gram = each(lambda a, r_, b_, k_, g: _bdot(jnp.concatenate([a, bf(r_)], axis=0),
                                               jnp.concatenate([bd(bf(b_ * g)), bd(bf(k_ * g))], axis=0), _NT),
                at, rt, bb, kd, g_iv)
    a_ab = [bf(g[:L, :LANES]) for g in gram]
    a_ak = [bf(g[:L, LANES:]) * strict_b for g in gram]
    a_rbk = [bf(g[L:, :]) * incl2_b for g in gram]
    tinv = [c_ref[_C_EYE] + g[:L, :LANES] * c_ref[_C_LEVEL0] for g in gram]
    for lv in range(1, N_LEVELS):
        m_left, m_right = level_b[lv]
        tinv_b = [bf(t_) for t_ in tinv]
        tx = each(lambda t_, a: _bdot(t_, jnp.concatenate([a * m_left, a * m_right], axis=0), _NN), tinv_b, a_ab)
        tinv = each(lambda t_, tb_, x: t_ + _bdot(bf(x), bd(tb_), _NN), tinv, tinv_b, tx)
    av = each(lambda a, v_: _bdot(a, bd(v_), _NN), a_ak, v)
    tz = each(lambda t_, a, x: bf(_bdot(bf(t_), jnp.concatenate([bd(a), bd(bf(x))], axis=1), _NN)), tinv, at, av)
    wt = [x[:, :LANES] for x in tz]
    u0 = [x[:, LANES:] for x in tz]
    zero_bd = jnp.zeros((LANES, LANES), _BF16)
    qy = each(lambda a, w, u, v_: _bdot(
        a, jnp.concatenate([jnp.concatenate([bd(w), bd(u)], axis=1),
                            jnp.concatenate([zero_bd, bd(v_)], axis=1)], axis=0), _NN),
        a_rbk, wt, u0, v)
    qq = each(lambda r_, x: bf(r_ + x[:, :LANES]), rt, qy)
    y0 = [x[:, LANES:] for x in qy]
    pm = each(lambda w, b_: bf(_bdot(w, b_, _TN)) * bdmask_b, wt, bh)
    cp = each(lambda u, v_, b_, k_: _bdot(jnp.concatenate([u, v_], axis=0), jnp.concatenate([b_, k_], axis=0),
                                          _TN) * bdmask, u0, v, bh, kh)
    gl = [jnp.exp(x) for x in tot]

    for n, (q, i) in enumerate(chains):
        s0 = s_scr[q]
        s0_b = bf(s0)
        y_ref[0, i * L:(i + 1) * L, q * LANES:(q + 1) * LANES] = _bdot(qq[n], s0_b, _NT) + y0[n]
        s_scr[q] = s0 * gl[n] + _bdot(s0_b, pm[n], _NN) + cp[n]

    @pl.when(j == pl.num_programs(2) - 1)
    def _():
        for q in range(npp):
            s = s_scr[q]
            st_ref[0, 2 * q] = s[:N, :N]
            st_ref[0, 2 * q + 1] = s[N:, N:]


def wkv_scan(proj, p, di, s0, reverse):
    b, t, _ = proj.shape
    d = D_RWKV
    nsub = CHUNKS_PER_STEP
    npp = PAIRS_PER_STEP
    tb = nsub * CHUNK
    width = LANES * npp
    nblk = t // tb
    ngrp = d // width
    halo = 8
    consts = jnp.asarray(_scan_consts(reverse))
    pad_rows = lambda w: jnp.zeros((LANES, d), _F32).at[di * w.shape[0]:(di + 1) * w.shape[0]].set(w)
    wup = pad_rows(p["w_decay_up"][di])
    aup = pad_rows(p["a_up"][di])
    row = lambda a: a.reshape(1, d)

    tmap = (lambda j: nblk - 1 - j) if reverse else (lambda j: j)

    def seq(col0):
        return pl.BlockSpec((1, tb, width), lambda i, g, j: (i, tmap(j), col0 // width + g))

    def prev(col0):
        return pl.BlockSpec((1, halo, width),
                            lambda i, g, j: (i, jnp.maximum(tmap(j) * (tb // halo) - 1, 0), col0 // width + g))

    def nxt(col0):
        return pl.BlockSpec((1, halo, width),
                            lambda i, g, j: (i, jnp.minimum((tmap(j) + 1) * (tb // halo), t // halo - 1),
                                             col0 // width + g))

    def lowrank(col0):
        return pl.BlockSpec((1, tb, LANES), lambda i, g, j: (i, tmap(j), col0 // LANES))

    def cols(nrows, col0=0):
        return pl.BlockSpec((nrows, width), lambda i, g, j: (0, col0 // width + g))

    out_seq = pl.BlockSpec((1, tb, width), lambda i, g, j: (i, tmap(j), g))
    st_spec = pl.BlockSpec((1, 2 * npp, HEAD_DIM, HEAD_DIM), lambda i, g, j: (i, g, 0, 0))
    in_specs = ([pl.BlockSpec(consts.shape, lambda i, g, j: (0, 0, 0))]
                + [seq(P_R), seq(P_K), seq(P_V), prev(P_R), prev(P_K), prev(P_V), nxt(P_R), nxt(P_K), nxt(P_V)]
                + [cols(SHORT_CONV, 0), cols(SHORT_CONV, d), cols(SHORT_CONV, 2 * d)]
                + [lowrank(P_WDN), lowrank(P_ADN), cols(LANES), cols(LANES)]
                + [cols(1)] * 5 + [st_spec])
    y, bonus, st = pl.pallas_call(
        functools.partial(_wkv_kernel, nsub=nsub, npp=npp, reverse=reverse),
        grid=(b, ngrp, nblk),
        in_specs=in_specs,
        out_specs=[out_seq, out_seq, st_spec],
        out_shape=[jax.ShapeDtypeStruct((b, t, d), _F32),
                   jax.ShapeDtypeStruct((b, t, d), _F32),
                   jax.ShapeDtypeStruct(s0.shape, _F32)],
        scratch_shapes=[pltpu.VMEM((npp, LANES, LANES), _F32)],
        compiler_params=pltpu.CompilerParams(
            dimension_semantics=("parallel", "parallel", "arbitrary"), vmem_limit_bytes=VMEM_LIMIT),
        name="wkv_scan_bwd" if reverse else "wkv_scan_fwd",
    )(consts, proj, proj, proj, proj, proj, proj, proj, proj, proj,
      p["conv_w"], p["conv_w"], p["conv_w"], proj, proj, wup, aup,
      row(p["w_decay0"][di]), row(p["a0"][di]), row(p["k_k"]), row(p["k_a"]), row(p["r_k"]), s0)
    return y, bonus, st


def _layer_norm_rows(x, g, b):
    mu = jnp.mean(x, axis=-1, keepdims=True)
    xc = x - mu
    var = jnp.mean(xc * xc, axis=-1, keepdims=True)
    return xc * lax.rsqrt(var + LN_EPS) * g + b


def _post_kernel(yf_ref, yb_ref, bf_ref, bb_ref, gdn_ref, gf_ref, gr_ref, fo_ref, x_ref,
                 g1_ref, sc2_ref, sh2_ref, gup_ref, gng_ref, gnb_ref, wfo_ref, wro_ref, wout_ref,
                 l1g_ref, l1b_ref, wr_ref, bd_ref, x1_ref, h2_ref, lg_ref):
    bdm = bd_ref[...]

    def segsum(x):
        x_h = x.astype(_BF16)
        x_l = (x - x_h.astype(_F32)).astype(_BF16)
        return jnp.dot(x_h, bdm, preferred_element_type=_F32) + jnp.dot(x_l, bdm, preferred_element_type=_F32)

    y = yf_ref[0] + yb_ref[0]
    parts = []
    for q in range(D_RWKV // LANES):
        yq = y[:, q * LANES:(q + 1) * LANES]
        dq = yq - segsum(yq) * (1.0 / HEAD_DIM)
        parts.append(dq * lax.rsqrt(segsum(dq * dq) * (1.0 / HEAD_DIM) + GN_EPS))
    yn = jnp.concatenate(parts, axis=1) * gng_ref[...] + gnb_ref[...]
    g = jnp.dot(jax.nn.sigmoid(gdn_ref[0]).astype(_BF16), gup_ref[...], preferred_element_type=_F32)
    z = (yn + bf_ref[0] + bb_ref[0]) * g
    o_r = jnp.dot(z.astype(_BF16), wro_ref[...], preferred_element_type=_F32)
    o_f = jnp.dot(fo_ref[0].astype(_BF16), wfo_ref[...], preferred_element_type=_F32)
    merged = jax.nn.sigmoid(gf_ref[0]) * o_f + jax.nn.sigmoid(gr_ref[0]) * o_r
    mix = jnp.dot(merged.astype(_BF16), wout_ref[...], preferred_element_type=_F32)
    x1 = _layer_norm_rows(DEEPNORM_ALPHA * x_ref[0] + g1_ref[0] * mix, l1g_ref[...], l1b_ref[...])
    x1_ref[0] = x1
    h2 = x1 * (1.0 + sc2_ref[0]) + sh2_ref[0]
    h2_ref[0] = h2.astype(_BF16)
    lg_ref[0] = jnp.dot(h2, wr_ref[...], precision=_HI, preferred_element_type=_F32)


def post_mix(y_f, y_b, bo_f, bo_b, proj, fourier, x, g1, sc2, sh2, p, *, tm=256):
    b, t, d = x.shape
    tm = min(tm, t)
    bdm = jnp.asarray(np.kron(np.eye(2), np.ones((HEAD_DIM, HEAD_DIM))), _BF16)
    tok = lambda w, c0=0: pl.BlockSpec((1, tm, w), lambda i, j: (i, j, c0 // w))
    per_seq = pl.BlockSpec((1, 1, d), lambda i, j: (i, 0, 0))
    full = lambda a: pl.BlockSpec(a.shape, lambda i, j: (0,) * a.ndim)
    row = lambda a: a.reshape(1, -1)
    weights = [p["g_up"].astype(_BF16), row(p["gn_g"]), row(p["gn_b"]), p["w_fo"].astype(_BF16),
               p["w_ro"].astype(_BF16), p["w_out"].astype(_BF16), row(p["ln1_g"]), row(p["ln1_b"]),
               p["w_router"], bdm]
    return pl.pallas_call(
        _post_kernel,
        grid=(b, t // tm),
        in_specs=[tok(d), tok(d), tok(d), tok(d), tok(LORA_G, P_GDN), tok(d, P_GATE_F), tok(d, P_GATE_R),
                  tok(D_FOURIER), tok(d), per_seq, per_seq, per_seq] + [full(w) for w in weights],
        out_specs=[tok(d), tok(d), tok(N_EXPERTS)],
        out_shape=[jax.ShapeDtypeStruct((b, t, d), _F32),
                   jax.ShapeDtypeStruct((b, t, d), _BF16),
                   jax.ShapeDtypeStruct((b, t, N_EXPERTS), _F32)],
        compiler_params=pltpu.CompilerParams(
            dimension_semantics=("parallel", "parallel"), vmem_limit_bytes=VMEM_LIMIT),
        name="post_mix",
    )(y_f, y_b, bo_f, bo_b, proj, proj, proj, fourier, x, g1, sc2, sh2, *weights)


def _expert_kernel(x_ref, g_ref, w1_ref, w3_ref, w2_ref, o_ref):
    f = pl.program_id(2)
    x = x_ref[0].astype(_BF16)
    h1 = jnp.dot(x, w1_ref[0].astype(_BF16), preferred_element_type=_F32)
    h3 = jnp.dot(x, w3_ref[0].astype(_BF16), preferred_element_type=_F32)
    he = (h1 * jax.nn.sigmoid(h1)) * h3
    part = jnp.dot(he.astype(_BF16), w2_ref[0].astype(_BF16), preferred_element_type=_F32)

    @pl.when(f == 0)
    def _():
        o_ref[0] = part

    @pl.when(f > 0)
    def _():
        o_ref[0] += part

    @pl.when(f == pl.num_programs(2) - 1)
    def _():
        o_ref[0] = o_ref[0] * g_ref[0]


def expert_ffn(xe, gate, w1, w3, w2, *, tm=1024, tf=512):
    e, c, d = xe.shape
    f = w1.shape[2]
    tm = min(tm, c)
    return pl.pallas_call(
        _expert_kernel,
        grid=(e, c // tm, f // tf),
        in_specs=[pl.BlockSpec((1, tm, d), lambda i, j, k: (i, j, 0)),
                  pl.BlockSpec((1, tm, 1), lambda i, j, k: (i, j, 0)),
                  pl.BlockSpec((1, d, tf), lambda i, j, k: (i, 0, k)),
                  pl.BlockSpec((1, d, tf), lambda i, j, k: (i, 0, k)),
                  pl.BlockSpec((1, tf, d), lambda i, j, k: (i, k, 0))],
        out_specs=pl.BlockSpec((1, tm, d), lambda i, j, k: (i, j, 0)),
        out_shape=jax.ShapeDtypeStruct((e, c, d), _F32),
        compiler_params=pltpu.CompilerParams(
            dimension_semantics=("parallel", "parallel", "arbitrary"), vmem_limit_bytes=VMEM_LIMIT),
        name="expert_ffn",
    )(xe, gate, w1, w3, w2)


COMBINE_WINDOW = 256
COMBINE_TOKENS = 512


def _combine_kernel(starts_ref, idx_ref, ye_hbm, x1_ref, g2_ref, lg_ref, lb_ref, o_ref, buf, acc, sem,
                    *, tm, cap, win):
    j = pl.program_id(0)
    n_tiles = pl.num_programs(0)
    n_exp = idx_ref.shape[0]
    spare = n_exp
    tok = j * tm + lax.broadcasted_iota(jnp.int32, (tm, 1), 0)

    def first_row(e, tile):
        return (starts_ref[e, tile] // LANES) * LANES

    def window_start(e, k, tile):
        return pl.multiple_of(jnp.minimum(first_row(e, tile) + k * win, cap - win), LANES)

    def n_windows(e):
        return jnp.maximum((starts_ref[e, j + 1] - first_row(e, j) + win - 1) // win, 1)

    def copy(e, k, tile, slot):
        rows = pl.ds(window_start(e, k, tile), win)
        return pltpu.make_async_copy(ye_hbm.at[e, rows, :], buf.at[slot], sem.at[slot])

    def place(e, k, slot):
        ids = idx_ref[e:e + 1, pl.ds(window_start(e, k, j), win)]
        row_of = window_start(e, k, j) + lax.broadcasted_iota(jnp.int32, (1, win), 1)
        ids = jnp.where(row_of >= first_row(e, j) + k * win, ids, -1)
        onehot = (tok == ids).astype(_BF16)
        acc[...] += jnp.dot(jnp.concatenate([onehot, onehot], axis=1), jnp.concatenate(_split(buf[slot]), axis=0),
                            preferred_element_type=_F32)

    @pl.when(j == 0)
    def _():
        for e in range(n_exp):
            copy(e, 0, 0, e).start()

    acc[...] = jnp.zeros_like(acc)
    for e in range(n_exp):
        copy(e, 0, j, e).wait()
        place(e, 0, e)

        @pl.when(j + 1 < n_tiles)
        def _(e=e):
            copy(e, 0, j + 1, e).start()

        def further(k, carry, e=e):
            cp = copy(e, k, j, spare)
            cp.start()
            cp.wait()
            place(e, k, spare)
            return carry

        lax.fori_loop(1, n_windows(e), further, 0)

    x = DEEPNORM_ALPHA * x1_ref[...] + g2_ref[0] * acc[...]
    o_ref[...] = _layer_norm_rows(x, lg_ref[...], lb_ref[...])


def combine(ye, idx_s, x1, g2, ln_g, ln_b):
    n_exp, cap, d = ye.shape
    b, t, _ = x1.shape
    n = b * t
    tm = min(COMBINE_TOKENS, t)
    win = min(COMBINE_WINDOW, cap)
    ntiles = n // tm
    bounds = jnp.arange(ntiles + 1, dtype=jnp.int32) * tm
    starts = jnp.sum(idx_s[:, :, None] < bounds[None, None, :], axis=1).astype(jnp.int32)
    row = lambda a: a.reshape(1, d)
    grid_spec = pltpu.PrefetchScalarGridSpec(
        num_scalar_prefetch=1,
        grid=(ntiles,),
        in_specs=[pl.BlockSpec((n_exp, cap), lambda j, st: (0, 0)),
                  pl.BlockSpec(memory_space=pl.ANY),
                  pl.BlockSpec((tm, d), lambda j, st: (j, 0)),
                  pl.BlockSpec((1, 1, d), lambda j, st: (j * tm // t, 0, 0)),
                  pl.BlockSpec((1, d), lambda j, st: (0, 0)),
                  pl.BlockSpec((1, d), lambda j, st: (0, 0))],
        out_specs=pl.BlockSpec((tm, d), lambda j, st: (j, 0)),
        scratch_shapes=[pltpu.VMEM((n_exp + 1, win, d), _F32), pltpu.VMEM((tm, d), _F32),
                        pltpu.SemaphoreType.DMA((n_exp + 1,))],
    )
    out = pl.pallas_call(
        functools.partial(_combine_kernel, tm=tm, cap=cap, win=win),
        grid_spec=grid_spec,
        out_shape=jax.ShapeDtypeStruct((n, d), _F32),
        compiler_params=pltpu.CompilerParams(dimension_semantics=("arbitrary",), vmem_limit_bytes=VMEM_LIMIT),
        name="combine",
    )(starts, idx_s, ye, x1.reshape(n, d), g2, row(ln_g), row(ln_b))
    return out.reshape(b, t, d)


def _expert_choice_ffn(h2, logits, p):
    b, t, d = h2.shape
    n = b * t
    cap = CAPACITY_FACTOR * n // N_EXPERTS
    hf = h2.reshape(n, d)
    aff = jax.nn.softmax(logits.reshape(n, N_EXPERTS), axis=-1)
    gate, idx = lax.top_k(aff.T, cap)
    idx_s, gate_s = lax.sort_key_val(idx, gate, dimension=1)
    ye = expert_ffn(hf[idx_s], gate_s[..., None], p["w_e1"], p["w_e3"], p["w_e2"])
    return ye, idx_s


def _trunk_layer(x, cond, s_f0, s_b0, p):
    mod = matmul(jax.nn.silu(cond), p["w_ada"], precise=True) + p["b_ada"]
    sh1, sc1, g1, sh2, sc2, g2 = [a[:, None, :] for a in jnp.split(mod, 6, axis=-1)]
    proj = in_projection(x, sc1, sh1, p["w_in_perm"])
    fourier = fourier_mix(proj)
    y_f, bo_f, s_f = wkv_scan(proj, p, 0, s_f0, False)
    y_b, bo_b, s_b = wkv_scan(proj, p, 1, s_b0, True)
    x1, h2, logits = post_mix(y_f, y_b, bo_f, bo_b, proj, fourier, x, g1, sc2, sh2, p)
    ye, idx_s = _expert_choice_ffn(h2, logits, p)
    x2 = combine(ye, idx_s, x1, g2, p["ln2_g"], p["ln2_b"])
    return x2, s_f, s_b


def kernel(x_prompt, x_sample, state_fwd, state_bwd, c, c_ctx, w_ada, b_ada, w_in, conv_w, w_decay0, w_decay_up, a0, a_up, g_up, k_k, k_a, r_k, gn_g, gn_b, w_fo, w_ro, w_out, ln1_g, ln1_b, w_router, w_e1, w_e3, w_e2, ln2_g, ln2_b):
    n_ctx = x_prompt.shape[0]
    cond_ctx = jnp.broadcast_to(c_ctx, (n_ctx, D_MODEL))
    zero_state = jnp.zeros((n_ctx, N_HEADS, HEAD_DIM, HEAD_DIM), _F32)
    names = ("w_ada", "b_ada", "w_in", "conv_w", "w_decay0", "w_decay_up", "a0", "a_up", "g_up", "k_k", "k_a",
             "r_k", "gn_g", "gn_b", "w_fo", "w_ro", "w_out", "ln1_g", "ln1_b", "w_router", "w_e1", "w_e3",
             "w_e2", "ln2_g", "ln2_b")
    stacked = (w_ada, b_ada, w_in, conv_w, w_decay0, w_decay_up, a0, a_up, g_up, k_k, k_a, r_k, gn_g, gn_b,
               w_fo, w_ro, w_out, ln1_g, ln1_b, w_router, w_e1, w_e3, w_e2, ln2_g, ln2_b)
    xp, xs = x_prompt, x_sample
    new_f, new_b = [], []
    for l in range(DEPTH):
        p = {k: a[l] for k, a in zip(names, stacked)}
        p["w_in_perm"] = _permute_in_weight(p["w_in"]).astype(_BF16)
        xp, s_f, s_b = _trunk_layer(xp, cond_ctx, zero_state, zero_state, p)
        new_f.append(s_f)
        new_b.append(s_b)
        xs, _, _ = _trunk_layer(xs, c, state_fwd[:, l], state_bwd[:, l], p)
    return (xp, xs, jnp.stack(new_f, axis=1), jnp.stack(new_b, axis=1))
```

```python
import functools

import numpy as np
import jax
import jax.numpy as jnp
from jax import lax
from jax.experimental import pallas as pl
from jax.experimental.pallas import tpu as pltpu

D_MODEL = 1024
N_FOURIER_GROUPS = 4
FOURIER_GROUP_DIM = 128
D_FOURIER = N_FOURIER_GROUPS * FOURIER_GROUP_DIM
HEAD_DIM = 64
D_RWKV = D_MODEL
N_HEADS = D_RWKV // HEAD_DIM
LORA_W = 64
LORA_A = 64
LORA_G = 128
N_DIR = 2
SHORT_CONV = 3
GN_EPS = 64e-5
N_EXPERTS = 16
CAPACITY_FACTOR = 2
D_EXPERT = 2048
LN_EPS = 1e-5
DEPTH = 1
DEEPNORM_ALPHA = (2.0 * DEPTH) ** 0.25
IN_WIDTHS = (D_FOURIER, 3 * D_RWKV, N_DIR * LORA_W, N_DIR * LORA_A, LORA_G, D_MODEL, D_MODEL)
D_IN = sum(IN_WIDTHS)
SPLIT_POINTS = tuple(sum(IN_WIDTHS[:i + 1]) for i in range(len(IN_WIDTHS) - 1))

LANES = 128
CHUNK = HEAD_DIM
CHUNKS_PER_STEP = 8
PAIRS_PER_STEP = 4
VMEM_LIMIT = 48 * 1024 * 1024

_HI = lax.Precision.HIGHEST
_F32 = jnp.float32
_BF16 = jnp.bfloat16


def _dot(a, b, precise):
    if precise:
        return jnp.dot(a, b, precision=_HI, preferred_element_type=_F32)
    return jnp.dot(a.astype(_BF16), b.astype(_BF16), preferred_element_type=_F32)


def _mm_kernel(a_ref, b_ref, o_ref, *, precise):
    o_ref[...] = _dot(a_ref[...], b_ref[...], precise)


def matmul(a, b, *, precise=False, tm=512, tn=512):
    m, k = a.shape
    n = b.shape[1]
    tm = min(tm, m)
    tn = min(tn, n)
    return pl.pallas_call(
        functools.partial(_mm_kernel, precise=precise),
        grid=(pl.cdiv(m, tm), pl.cdiv(n, tn)),
        in_specs=[pl.BlockSpec((tm, k), lambda i, j: (i, 0)),
                  pl.BlockSpec((k, tn), lambda i, j: (0, j))],
        out_specs=pl.BlockSpec((tm, tn), lambda i, j: (i, j)),
        out_shape=jax.ShapeDtypeStruct((m, n), _F32),
        compiler_params=pltpu.CompilerParams(
            dimension_semantics=("parallel", "parallel"), vmem_limit_bytes=VMEM_LIMIT),
    )(a, b)


P_R, P_K, P_V = 0, D_RWKV, 2 * D_RWKV
P_GATE_F = 3 * D_RWKV
P_GATE_R = P_GATE_F + D_MODEL
P_UF = P_GATE_R + D_MODEL
P_WDN = P_UF + D_FOURIER
P_ADN = P_WDN + N_DIR * LORA_W
P_GDN = P_ADN + N_DIR * LORA_A
D_PROJ = P_GDN + LORA_G + 128
assert D_PROJ % 512 == 0 and N_DIR * LORA_W == LANES and N_DIR * LORA_A == LANES and LORA_G == LANES


def _permute_in_weight(w_in):
    s = (0,) + SPLIT_POINTS + (D_IN,)
    seg = [w_in[:, s[i]:s[i + 1]] for i in range(len(IN_WIDTHS))]
    pad = jnp.zeros((w_in.shape[0], D_PROJ - D_IN), w_in.dtype)
    return jnp.concatenate([seg[1], seg[5], seg[6], seg[0], seg[2], seg[3], seg[4], pad], axis=1)


def _inproj_kernel(x_ref, sc_ref, sh_ref, w_ref, o_ref, h_scr):
    nb, tm, d = x_ref.shape

    @pl.when(pl.program_id(2) == 0)
    def _():
        h = x_ref[...] * (1.0 + sc_ref[...]) + sh_ref[...]
        h_scr[...] = h.reshape(nb * tm, d).astype(_BF16)

    out = jnp.dot(h_scr[...], w_ref[...], preferred_element_type=_F32)
    o_ref[...] = out.reshape(nb, tm, out.shape[-1])


def in_projection(x, sc, sh, w, *, rows=1024, tn=512):
    b, t, d = x.shape
    n = w.shape[1]
    tm = min(rows, t)
    nb = max(1, min(rows // tm, b))
    while b % nb:
        nb -= 1
    assert t % tm == 0
    return pl.pallas_call(
        _inproj_kernel,
        grid=(b // nb, t // tm, n // tn),
        in_specs=[pl.BlockSpec((nb, tm, d), lambda i, j, k: (i, j, 0)),
                  pl.BlockSpec((nb, 1, d), lambda i, j, k: (i, 0, 0)),
                  pl.BlockSpec((nb, 1, d), lambda i, j, k: (i, 0, 0)),
                  pl.BlockSpec((d, tn), lambda i, j, k: (0, k))],
        out_specs=pl.BlockSpec((nb, tm, tn), lambda i, j, k: (i, j, k)),
        out_shape=jax.ShapeDtypeStruct((b, t, n), _F32),
        scratch_shapes=[pltpu.VMEM((nb * tm, d), _BF16)],
        compiler_params=pltpu.CompilerParams(
            dimension_semantics=("parallel", "parallel", "arbitrary"), vmem_limit_bytes=VMEM_LIMIT),
        name="in_projection",
    )(x, sc, sh, w)


FOURIER_UNROLL = 8


def _dft_cos_sin(n):
    idx = np.arange(n)
    ang = 2.0 * np.pi * ((idx[:, None] * idx[None, :]) % n) / n
    return np.cos(ang), np.sin(ang)


def _split(x):
    x_h = x.astype(_BF16)
    return x_h, (x - x_h.astype(_F32)).astype(_BF16)


def _dot3(a, b):
    dg = lambda x, y: jnp.dot(x, y, preferred_element_type=_F32)
    return dg(a[0], b[0]) + dg(a[0], b[1]) + dg(a[1], b[0])


def _fourier_dense_kernel(u_ref, wc_ref, wt_ref, o_ref):
    t, gd = u_ref.shape[1], FOURIER_GROUP_DIM
    z = _dot3(_split(u_ref[0]), _split(wc_ref[...]))
    m = _dot3(_split(wt_ref[...]), _split(z))
    o_ref[0] = m[:t, :gd] - m[t:, gd:]


def _fourier_factored_kernel(u_ref, wc_ref, w1_ref, twc_ref, tws_ref, w2_ref, o_ref, p_scr, q_scr, a_scr,
                             *, n1):
    gd = FOURIER_GROUP_DIM
    z = _dot3(_split(u_ref[0]), _split(wc_ref[...]))
    p_scr[...] = z[:, :gd]
    q_scr[...] = z[:, gd:]
    w1 = _split(w1_ref[...])
    w2 = _split(w2_ref[...])

    def stage1(step, carry):
        t2s = [step * FOURIER_UNROLL + u for u in range(FOURIER_UNROLL)]
        rows = [pl.ds(t2, n1, stride=n1) for t2 in t2s]
        ms = [_dot3(w1, _split(jnp.concatenate([p_scr[r, :], q_scr[r, :]], axis=1))) for r in rows]
        for t2, m in zip(t2s, ms):
            x1r = m[:n1, :gd] - m[n1:, gd:]
            x1i = -(m[:n1, gd:] + m[n1:, :gd])
            c = twc_ref[t2]
            s = tws_ref[t2]
            a_scr[pl.ds(t2, n1, stride=2 * n1), :] = x1r * c + x1i * s
            a_scr[pl.ds(n1 + t2, n1, stride=2 * n1), :] = x1i * c - x1r * s
        return carry

    lax.fori_loop(0, n1 // FOURIER_UNROLL, stage1, 0)

    def stage2(step, carry):
        t1s = [step * FOURIER_UNROLL + u for u in range(FOURIER_UNROLL)]
        outs = [_dot3(w2, _split(a_scr[pl.ds(pl.multiple_of(t1 * 2 * n1, 2 * n1), 2 * n1), :])) for t1 in t1s]
        for t1, o in zip(t1s, outs):
            o_ref[0, pl.ds(t1, n1, stride=n1), :] = o
        return carry

    lax.fori_loop(0, n1 // FOURIER_UNROLL, stage2, 0)


def fourier_mix(proj):
    b, t, _ = proj.shape
    gd = FOURIER_GROUP_DIM
    cc, sc = _dft_cos_sin(gd)
    w_c = jnp.asarray(np.concatenate([cc, sc], axis=1) / np.sqrt(gd), _F32)
    u_spec = pl.BlockSpec((1, t, gd), lambda i, g: (i, 0, P_UF // gd + g))
    o_spec = pl.BlockSpec((1, t, gd), lambda i, g: (i, 0, g))
    full = lambda a: pl.BlockSpec(a.shape, lambda i, g: (0,) * a.ndim)
    params = pltpu.CompilerParams(dimension_semantics=("parallel", "parallel"), vmem_limit_bytes=VMEM_LIMIT)
    out_shape = jax.ShapeDtypeStruct((b, t, D_FOURIER), _F32)
    if t <= 512:
        ct, st = _dft_cos_sin(t)
        w_t = jnp.asarray(np.concatenate([ct, st], axis=0) / np.sqrt(t), _F32)
        return pl.pallas_call(
            _fourier_dense_kernel, grid=(b, N_FOURIER_GROUPS),
            in_specs=[u_spec, full(w_c), full(w_t)], out_specs=o_spec, out_shape=out_shape,
            compiler_params=params, name="fourier_dense",
        )(proj, w_c, w_t)
    n1 = int(round(np.sqrt(t)))
    assert n1 * n1 == t and n1 % FOURIER_UNROLL == 0
    c1, s1 = _dft_cos_sin(n1)
    w1 = jnp.asarray(np.concatenate([c1, s1], axis=0) / np.sqrt(n1), _F32)
    w2 = jnp.asarray(np.concatenate([c1, s1], axis=1) / np.sqrt(n1), _F32)
    tw = 2.0 * np.pi * (np.arange(n1)[:, None] * np.arange(n1)[None, :]) / t
    lanes = np.ones((1, 1, gd))
    twc = jnp.asarray(np.cos(tw)[:, :, None] * lanes, _F32)
    tws = jnp.asarray(np.sin(tw)[:, :, None] * lanes, _F32)
    return pl.pallas_call(
        functools.partial(_fourier_factored_kernel, n1=n1), grid=(b, N_FOURIER_GROUPS),
        in_specs=[u_spec, full(w_c), full(w1), full(twc), full(tws), full(w2)],
        out_specs=o_spec, out_shape=out_shape,
        scratch_shapes=[pltpu.VMEM((t, gd), _F32), pltpu.VMEM((t, gd), _F32), pltpu.VMEM((2 * t, gd), _F32)],
        compiler_params=params, name="fourier_factored",
    )(proj, w_c, w1, twc, tws, w2)


N_LEVELS = int(np.log2(CHUNK))

_NN = (((1,), (0,)), ((), ()))
_NT = (((1,), (1,)), ((), ()))
_TN = (((0,), (0,)), ((), ()))

_C_INCL, _C_STRICT, _C_EYE, _C_LEVEL0 = 0, 1, 2, 3
_C_LEFT = _C_LEVEL0 + N_LEVELS
_C_RIGHT = _C_LEFT + 1


def _scan_consts(reverse):
    L = CHUNK
    t = np.arange(L)[:, None]
    s = np.arange(L)[None, :]
    if reverse:
        t, s = s, t
    strict = (s < t)
    incl = (s <= t)
    levels = []
    bs = 1
    while bs < L:
        levels.append((t // (2 * bs) == s // (2 * bs)) & (t % (2 * bs) >= bs) & (s % (2 * bs) < bs))
        bs *= 2
    pair = lambda m: np.concatenate([m, m], axis=1)
    left = np.concatenate([np.ones((L, HEAD_DIM)), np.zeros((L, HEAD_DIM))], axis=1)
    slabs = [pair(incl), pair(strict), pair(np.eye(L))] + [pair(l) for l in levels] + [left, 1.0 - left]
    return np.stack(slabs).astype(np.float32)


def _bdot(a, b, dims):
    return lax.dot_general(a, b, dims, preferred_element_type=_F32)


def _wkv_kernel(c_ref, r_ref, k_ref, v_ref, rp_ref, kp_ref, vp_ref, rn_ref, kn_ref, vn_ref,
                cwr_ref, cwk_ref, cwv_ref, wdn_ref, adn_ref, wup_ref, aup_ref, w0_ref, a0_ref,
                kkw_ref, kaw_ref, rkw_ref, s0_ref, y_ref, bo_ref, st_ref, s_scr, *, nsub, npp, reverse):
    L, N = CHUNK, HEAD_DIM
    tb = nsub * L
    j = pl.program_id(2)
    nblk = pl.num_programs(2)
    bf = lambda x: x.astype(_BF16)
    inclp = c_ref[_C_INCL]
    left = c_ref[_C_LEFT]
    right = c_ref[_C_RIGHT]
    left_b, right_b = bf(left), bf(right)
    bdmask = jnp.concatenate([left, right], axis=0)
    bdmask_b = bf(bdmask)
    incl_b = bf(inclp[:, :L])
    strict_b = bf(c_ref[_C_STRICT])
    incl2_b = bf(jnp.concatenate([inclp, inclp], axis=1))
    level_b = [(bf(c_ref[_C_LEVEL0 + lv] * left), bf(c_ref[_C_LEVEL0 + lv] * right)) for lv in range(N_LEVELS)]

    tblk = (nblk - 1 - j) if reverse else j
    has_prev = (tblk > 0).astype(_F32)
    has_next = (tblk < nblk - 1).astype(_F32)
    row = lax.broadcasted_iota(jnp.int32, (tb, 1), 0)

    def conv(u_ref, up_ref, un_ref, cw_ref):
        u = u_ref[0]
        before = jnp.where(row == 0, up_ref[0, 7:8, :] * has_prev, pltpu.roll(u, 1, 0))
        after = jnp.where(row == tb - 1, un_ref[0, 0:1, :] * has_next, pltpu.roll(u, tb - 1, 0))
        return before * cw_ref[0:1, :] + u * cw_ref[1:2, :] + after * cw_ref[2:3, :]

    def segsum(x):
        x_h = x.astype(_BF16)
        x_l = (x - x_h.astype(_F32)).astype(_BF16)
        return (jnp.dot(x_h, bdmask_b, preferred_element_type=_F32)
                + jnp.dot(x_l, bdmask_b, preferred_element_type=_F32))

    r_all = conv(r_ref, rp_ref, rn_ref, cwr_ref)
    k_all = conv(k_ref, kp_ref, kn_ref, cwk_ref)
    v_all = conv(v_ref, vp_ref, vn_ref, cwv_ref)
    w_up = jnp.dot(jnp.tanh(wdn_ref[0]).astype(_BF16), wup_ref[...].astype(_BF16), preferred_element_type=_F32)
    a_up = jnp.dot(adn_ref[0].astype(_BF16), aup_ref[...].astype(_BF16), preferred_element_type=_F32)
    z = -(w0_ref[...] + w_up)
    softplus = jnp.maximum(z, 0.0) + jnp.log(1.0 + jnp.exp(-jnp.abs(z)))
    lw_all = -jnp.exp(-softplus - 0.5)
    a_rate = jax.nn.sigmoid(a0_ref[...] + a_up)
    kd_all = k_all * (1.0 + (a_rate - 1.0) * kaw_ref[...])
    kkr = k_all * kkw_ref[...]
    rkd = r_all * kd_all * rkw_ref[...]
    kk_parts, bonus_parts = [], []
    for q in range(npp):
        sl = slice(q * LANES, (q + 1) * LANES)
        kq = kkr[:, sl]
        kk_parts.append(kq * lax.rsqrt(segsum(kq * kq) + 1e-12))
        bonus_parts.append(segsum(rkd[:, sl]) * v_all[:, sl])
    bo_ref[0] = jnp.concatenate(bonus_parts, axis=1)
    kk_all = jnp.concatenate(kk_parts, axis=1)
    bb_all = kk_all * a_rate

    def bd(x):
        return jnp.concatenate([x * left_b, x * right_b], axis=0)

    @pl.when(j == 0)
    def _():
        zero = jnp.zeros((N, N), _F32)
        for q in range(npp):
            top = jnp.concatenate([s0_ref[0, 2 * q], zero], axis=1)
            bot = jnp.concatenate([zero, s0_ref[0, 2 * q + 1]], axis=1)
            s_scr[q] = jnp.concatenate([top, bot], axis=0)

    order = range(nsub - 1, -1, -1) if reverse else range(nsub)
    last_row = 0 if reverse else L - 1

    chains = [(q, i) for i in order for q in range(npp)]

    def each(fn, *lists):
        return [fn(*args) for args in zip(*lists)]

    def tile(x, c):
        q, i = c
        return x[i * L:(i + 1) * L, q * LANES:(q + 1) * LANES]

    lw = [tile(lw_all, c) for c in chains]

    def cumulative(x):
        x_h = x.astype(_BF16)
        rem = x - x_h.astype(_F32)
        x_m = rem.astype(_BF16)
        x_l = (rem - x_m.astype(_F32)).astype(_BF16)
        return (jnp.dot(incl_b, x_h, preferred_element_type=_F32)
                + jnp.dot(incl_b, x_m, preferred_element_type=_F32)
                + jnp.dot(incl_b, x_l, preferred_element_type=_F32))

    cum = each(cumulative, lw)
    tot = [x[last_row:last_row + 1, :] for x in cum]
    g_iv = [jnp.exp(-x) for x in cum]
    g_rem = each(lambda t_, x: jnp.exp(t_ - x), tot, cum)
    v = [bf(tile(v_all, c)) for c in chains]
    rt = [tile(r_all, c) * jnp.exp(x) for c, x in zip(chains, cum)]
    at = [bf(-tile(kk_all, c) * jnp.exp(x - w)) for c, x, w in zip(chains, cum, lw)]
    kd = [tile(kd_all, c) for c in chains]
    bb = [tile(bb_all, c) for c in chains]
    bh = each(lambda x, g: bf(x * g), bb, g_rem)
    kh = each(lambda x, g: bf(x * g), kd, g_rem)
    gram = each(lambda a, r_, b_, k_, g: _bdot(jnp.concatenate([a, bf(r_)], axis=0),
                                               jnp.concatenate([bd(bf(b_ * g)), bd(bf(k_ * g))], axis=0), _NT),
                at, rt, bb, kd, g_iv)
    a_ab = [bf(g[:L, :LANES]) for g in gram]
    a_ak = [bf(g[:L, LANES:]) * strict_b for g in gram]
    a_rbk = [bf(g[L:, :]) * incl2_b for g in gram]
    tinv = [c_ref[_C_EYE] + g[:L, :LANES] * c_ref[_C_LEVEL0] for g in gram]
    for lv in range(1, N_LEVELS):
        m_left, m_right = level_b[lv]
        tinv_b = [bf(t_) for t_ in tinv]
        tx = each(lambda t_, a: _bdot(t_, jnp.concatenate([a * m_left, a * m_right], axis=0), _NN), tinv_b, a_ab)
        tinv = each(lambda t_, tb_, x: t_ + _bdot(bf(x), bd(tb_), _NN), tinv, tinv_b, tx)
    av = each(lambda a, v_: _bdot(a, bd(v_), _NN), a_ak, v)
    tz = each(lambda t_, a, x: bf(_bdot(bf(t_), jnp.concatenate([bd(a), bd(bf(x))], axis=1), _NN)), tinv, at, av)
    wt = [x[:, :LANES] for x in tz]
    u0 = [x[:, LANES:] for x in tz]
    zero_bd = jnp.zeros((LANES, LANES), _BF16)
    qy = each(lambda a, w, u, v_: _bdot(
        a, jnp.concatenate([jnp.concatenate([bd(w), bd(u)], axis=1),
                            jnp.concatenate([zero_bd, bd(v_)], axis=1)], axis=0), _NN),
        a_rbk, wt, u0, v)
    qq = each(lambda r_, x: bf(r_ + x[:, :LANES]), rt, qy)
    y0 = [x[:, LANES:] for x in qy]
    pm = each(lambda w, b_: bf(_bdot(w, b_, _TN)) * bdmask_b, wt, bh)
    cp = each(lambda u, v_, b_, k_: _bdot(jnp.concatenate([u, v_], axis=0), jnp.concatenate([b_, k_], axis=0),
                                          _TN) * bdmask, u0, v, bh, kh)
    gl = [jnp.exp(x) for x in tot]

    for n, (q, i) in enumerate(chains):
        s0 = s_scr[q]
        s0_b = bf(s0)
        y_ref[0, i * L:(i + 1) * L, q * LANES:(q + 1) * LANES] = _bdot(qq[n], s0_b, _NT) + y0[n]
        s_scr[q] = s0 * gl[n] + _bdot(s0_b, pm[n], _NN) + cp[n]

    @pl.when(j == pl.num_programs(2) - 1)
    def _():
        for q in range(npp):
            s = s_scr[q]
            st_ref[0, 2 * q] = s[:N, :N]
            st_ref[0, 2 * q + 1] = s[N:, N:]


def wkv_scan(proj, p, di, s0, reverse):
    b, t, _ = proj.shape
    d = D_RWKV
    nsub = min(CHUNKS_PER_STEP, t // CHUNK)
    npp = PAIRS_PER_STEP
    tb = nsub * CHUNK
    width = LANES * npp
    nblk = t // tb
    ngrp = d // width
    halo = 8
    consts = jnp.asarray(_scan_consts(reverse))
    pad_rows = lambda w: jnp.zeros((LANES, d), _F32).at[di * w.shape[0]:(di + 1) * w.shape[0]].set(w)
    wup = pad_rows(p["w_decay_up"][di])
    aup = pad_rows(p["a_up"][di])
    row = lambda a: a.reshape(1, d)

    tmap = (lambda j: nblk - 1 - j) if reverse else (lambda j: j)

    def seq(col0):
        return pl.BlockSpec((1, tb, width), lambda i, g, j: (i, tmap(j), col0 // width + g))

    def prev(col0):
        return pl.BlockSpec((1, halo, width),
                            lambda i, g, j: (i, jnp.maximum(tmap(j) * (tb // halo) - 1, 0), col0 // width + g))

    def nxt(col0):
        return pl.BlockSpec((1, halo, width),
                            lambda i, g, j: (i, jnp.minimum((tmap(j) + 1) * (tb // halo), t // halo - 1),
                                             col0 // width + g))

    def lowrank(col0):
        return pl.BlockSpec((1, tb, LANES), lambda i, g, j: (i, tmap(j), col0 // LANES))

    def cols(nrows, col0=0):
        return pl.BlockSpec((nrows, width), lambda i, g, j: (0, col0 // width + g))

    out_seq = pl.BlockSpec((1, tb, width), lambda i, g, j: (i, tmap(j), g))
    st_spec = pl.BlockSpec((1, 2 * npp, HEAD_DIM, HEAD_DIM), lambda i, g, j: (i, g, 0, 0))
    in_specs = ([pl.BlockSpec(consts.shape, lambda i, g, j: (0, 0, 0))]
                + [seq(P_R), seq(P_K), seq(P_V), prev(P_R), prev(P_K), prev(P_V), nxt(P_R), nxt(P_K), nxt(P_V)]
                + [cols(SHORT_CONV, 0), cols(SHORT_CONV, d), cols(SHORT_CONV, 2 * d)]
                + [lowrank(P_WDN), lowrank(P_ADN), cols(LANES), cols(LANES)]
                + [cols(1)] * 5 + [st_spec])
    y, bonus, st = pl.pallas_call(
        functools.partial(_wkv_kernel, nsub=nsub, npp=npp, reverse=reverse),
        grid=(b, ngrp, nblk),
        in_specs=in_specs,
        out_specs=[out_seq, out_seq, st_spec],
        out_shape=[jax.ShapeDtypeStruct((b, t, d), _F32),
                   jax.ShapeDtypeStruct((b, t, d), _F32),
                   jax.ShapeDtypeStruct(s0.shape, _F32)],
        scratch_shapes=[pltpu.VMEM((npp, LANES, LANES), _F32)],
        compiler_params=pltpu.CompilerParams(
            dimension_semantics=("parallel", "parallel", "arbitrary"), vmem_limit_bytes=VMEM_LIMIT),
        name="wkv_scan_bwd" if reverse else "wkv_scan_fwd",
    )(consts, proj, proj, proj, proj, proj, proj, proj, proj, proj,
      p["conv_w"], p["conv_w"], p["conv_w"], proj, proj, wup, aup,
      row(p["w_decay0"][di]), row(p["a0"][di]), row(p["k_k"]), row(p["k_a"]), row(p["r_k"]), s0)
    return y, bonus, st


def _layer_norm_rows(x, g, b):
    mu = jnp.mean(x, axis=-1, keepdims=True)
    xc = x - mu
    var = jnp.mean(xc * xc, axis=-1, keepdims=True)
    return xc * lax.rsqrt(var + LN_EPS) * g + b


def _post_kernel(yf_ref, yb_ref, bf_ref, bb_ref, gdn_ref, gf_ref, gr_ref, fo_ref, x_ref,
                 g1_ref, sc2_ref, sh2_ref, gup_ref, gng_ref, gnb_ref, wfo_ref, wro_ref, wout_ref,
                 l1g_ref, l1b_ref, wr_ref, bd_ref, x1_ref, h2_ref, lg_ref):
    bdm = bd_ref[...]

    def segsum(x):
        x_h = x.astype(_BF16)
        x_l = (x - x_h.astype(_F32)).astype(_BF16)
        return jnp.dot(x_h, bdm, preferred_element_type=_F32) + jnp.dot(x_l, bdm, preferred_element_type=_F32)

    y = yf_ref[0] + yb_ref[0]
    parts = []
    for q in range(D_RWKV // LANES):
        yq = y[:, q * LANES:(q + 1) * LANES]
        dq = yq - segsum(yq) * (1.0 / HEAD_DIM)
        parts.append(dq * lax.rsqrt(segsum(dq * dq) * (1.0 / HEAD_DIM) + GN_EPS))
    yn = jnp.concatenate(parts, axis=1) * gng_ref[...] + gnb_ref[...]
    g = jnp.dot(jax.nn.sigmoid(gdn_ref[0]).astype(_BF16), gup_ref[...], preferred_element_type=_F32)
    z = (yn + bf_ref[0] + bb_ref[0]) * g
    o_r = jnp.dot(z.astype(_BF16), wro_ref[...], preferred_element_type=_F32)
    o_f = jnp.dot(fo_ref[0].astype(_BF16), wfo_ref[...], preferred_element_type=_F32)
    merged = jax.nn.sigmoid(gf_ref[0]) * o_f + jax.nn.sigmoid(gr_ref[0]) * o_r
    mix = jnp.dot(merged.astype(_BF16), wout_ref[...], preferred_element_type=_F32)
    x1 = _layer_norm_rows(DEEPNORM_ALPHA * x_ref[0] + g1_ref[0] * mix, l1g_ref[...], l1b_ref[...])
    x1_ref[0] = x1
    h2 = x1 * (1.0 + sc2_ref[0]) + sh2_ref[0]
    h2_ref[0] = h2.astype(_BF16)
    lg_ref[0] = jnp.dot(h2, wr_ref[...], precision=_HI, preferred_element_type=_F32)


def post_mix(y_f, y_b, bo_f, bo_b, proj, fourier, x, g1, sc2, sh2, p, *, tm=256):
    b, t, d = x.shape
    tm = min(tm, t)
    bdm = jnp.asarray(np.kron(np.eye(2), np.ones((HEAD_DIM, HEAD_DIM))), _BF16)
    tok = lambda w, c0=0: pl.BlockSpec((1, tm, w), lambda i, j: (i, j, c0 // w))
    per_seq = pl.BlockSpec((1, 1, d), lambda i, j: (i, 0, 0))
    full = lambda a: pl.BlockSpec(a.shape, lambda i, j: (0,) * a.ndim)
    row = lambda a: a.reshape(1, -1)
    weights = [p["g_up"].astype(_BF16), row(p["gn_g"]), row(p["gn_b"]), p["w_fo"].astype(_BF16),
               p["w_ro"].astype(_BF16), p["w_out"].astype(_BF16), row(p["ln1_g"]), row(p["ln1_b"]),
               p["w_router"], bdm]
    return pl.pallas_call(
        _post_kernel,
        grid=(b, t // tm),
        in_specs=[tok(d), tok(d), tok(d), tok(d), tok(LORA_G, P_GDN), tok(d, P_GATE_F), tok(d, P_GATE_R),
                  tok(D_FOURIER), tok(d), per_seq, per_seq, per_seq] + [full(w) for w in weights],
        out_specs=[tok(d), tok(d), tok(N_EXPERTS)],
        out_shape=[jax.ShapeDtypeStruct((b, t, d), _F32),
                   jax.ShapeDtypeStruct((b, t, d), _BF16),
                   jax.ShapeDtypeStruct((b, t, N_EXPERTS), _F32)],
        compiler_params=pltpu.CompilerParams(
            dimension_semantics=("parallel", "parallel"), vmem_limit_bytes=VMEM_LIMIT),
        name="post_mix",
    )(y_f, y_b, bo_f, bo_b, proj, proj, proj, fourier, x, g1, sc2, sh2, *weights)


def _expert_kernel(x_ref, g_ref, w1_ref, w3_ref, w2_ref, o_ref, acc_ref):
    f = pl.program_id(2)
    x = x_ref[0].astype(_BF16)
    h1 = jnp.dot(x, w1_ref[0].astype(_BF16), preferred_element_type=_F32)
    h3 = jnp.dot(x, w3_ref[0].astype(_BF16), preferred_element_type=_F32)
    he = (h1 * jax.nn.sigmoid(h1)) * h3
    part = jnp.dot(he.astype(_BF16), w2_ref[0].astype(_BF16), preferred_element_type=_F32)

    @pl.when(f == 0)
    def _():
        acc_ref[...] = part

    @pl.when(f > 0)
    def _():
        acc_ref[...] += part

    @pl.when(f == pl.num_programs(2) - 1)
    def _():
        o_ref[0] = (acc_ref[...] * g_ref[0]).astype(o_ref.dtype)


def expert_ffn(xe, gate, w1, w3, w2, *, tm=1024, tf=512):
    e, c, d = xe.shape
    f = w1.shape[2]
    tm = min(tm, c)
    return pl.pallas_call(
        _expert_kernel,
        grid=(e, c // tm, f // tf),
        in_specs=[pl.BlockSpec((1, tm, d), lambda i, j, k: (i, j, 0)),
                  pl.BlockSpec((1, tm, 1), lambda i, j, k: (i, j, 0)),
                  pl.BlockSpec((1, d, tf), lambda i, j, k: (i, 0, k)),
                  pl.BlockSpec((1, d, tf), lambda i, j, k: (i, 0, k)),
                  pl.BlockSpec((1, tf, d), lambda i, j, k: (i, k, 0))],
        out_specs=pl.BlockSpec((1, tm, d), lambda i, j, k: (i, j, 0)),
        out_shape=jax.ShapeDtypeStruct((e, c, d), _BF16),
        scratch_shapes=[pltpu.VMEM((tm, d), _F32)],
        compiler_params=pltpu.CompilerParams(
            dimension_semantics=("parallel", "parallel", "arbitrary"), vmem_limit_bytes=VMEM_LIMIT),
        name="expert_ffn",
    )(xe, gate, w1, w3, w2)


COMBINE_WINDOW = 256
COMBINE_TOKENS = 512


def _combine_kernel(starts_ref, idx_ref, ye_hbm, x1_ref, g2_ref, lg_ref, lb_ref, o_ref, buf, acc, sem,
                    *, tm, cap, win):
    j = pl.program_id(0)
    n_tiles = pl.num_programs(0)
    n_exp = idx_ref.shape[0]
    spare = n_exp
    tok = j * tm + lax.broadcasted_iota(jnp.int32, (tm, 1), 0)

    def first_row(e, tile):
        return (starts_ref[e, tile] // LANES) * LANES

    def window_start(e, k, tile):
        return pl.multiple_of(jnp.minimum(first_row(e, tile) + k * win, cap - win), LANES)

    def n_windows(e):
        return jnp.maximum((starts_ref[e, j + 1] - first_row(e, j) + win - 1) // win, 1)

    def copy(e, k, tile, slot):
        rows = pl.ds(window_start(e, k, tile), win)
        return pltpu.make_async_copy(ye_hbm.at[e, rows, :], buf.at[slot], sem.at[slot])

    def place(e, k, slot):
        ids = idx_ref[e:e + 1, pl.ds(window_start(e, k, j), win)]
        row_of = window_start(e, k, j) + lax.broadcasted_iota(jnp.int32, (1, win), 1)
        ids = jnp.where(row_of >= first_row(e, j) + k * win, ids, -1)
        onehot = (tok == ids).astype(_BF16)
        acc[...] += jnp.dot(onehot, buf[slot], preferred_element_type=_F32)

    @pl.when(j == 0)
    def _():
        for e in range(n_exp):
            copy(e, 0, 0, e).start()

    acc[...] = jnp.zeros_like(acc)
    for e in range(n_exp):
        copy(e, 0, j, e).wait()
        place(e, 0, e)

        @pl.when(j + 1 < n_tiles)
        def _(e=e):
            copy(e, 0, j + 1, e).start()

        def further(k, carry, e=e):
            cp = copy(e, k, j, spare)
            cp.start()
            cp.wait()
            place(e, k, spare)
            return carry

        lax.fori_loop(1, n_windows(e), further, 0)

    x = DEEPNORM_ALPHA * x1_ref[...] + g2_ref[0] * acc[...]
    o_ref[...] = _layer_norm_rows(x, lg_ref[...], lb_ref[...])


def combine(ye, idx_s, x1, g2, ln_g, ln_b):
    n_exp, cap, d = ye.shape
    b, t, _ = x1.shape
    n = b * t
    tm = min(COMBINE_TOKENS, t)
    win = min(COMBINE_WINDOW, cap)
    ntiles = n // tm
    bounds = jnp.arange(ntiles + 1, dtype=jnp.int32) * tm
    starts = jnp.sum(idx_s[:, :, None] < bounds[None, None, :], axis=1).astype(jnp.int32)
    row = lambda a: a.reshape(1, d)
    grid_spec = pltpu.PrefetchScalarGridSpec(
        num_scalar_prefetch=1,
        grid=(ntiles,),
        in_specs=[pl.BlockSpec((n_exp, cap), lambda j, st: (0, 0)),
                  pl.BlockSpec(memory_space=pl.ANY),
                  pl.BlockSpec((tm, d), lambda j, st: (j, 0)),
                  pl.BlockSpec((1, 1, d), lambda j, st: (j * tm // t, 0, 0)),
                  pl.BlockSpec((1, d), lambda j, st: (0, 0)),
                  pl.BlockSpec((1, d), lambda j, st: (0, 0))],
        out_specs=pl.BlockSpec((tm, d), lambda j, st: (j, 0)),
        scratch_shapes=[pltpu.VMEM((n_exp + 1, win, d), ye.dtype), pltpu.VMEM((tm, d), _F32),
                        pltpu.SemaphoreType.DMA((n_exp + 1,))],
    )
    out = pl.pallas_call(
        functools.partial(_combine_kernel, tm=tm, cap=cap, win=win),
        grid_spec=grid_spec,
        out_shape=jax.ShapeDtypeStruct((n, d), _F32),
        compiler_params=pltpu.CompilerParams(dimension_semantics=("arbitrary",), vmem_limit_bytes=VMEM_LIMIT),
        name="combine",
    )(starts, idx_s, ye, x1.reshape(n, d), g2, row(ln_g), row(ln_b))
    return out.reshape(b, t, d)


def _expert_choice_ffn(h2, logits, p):
    b, t, d = h2.shape
    n = b * t
    cap = CAPACITY_FACTOR * n // N_EXPERTS
    hf = h2.reshape(n, d)
    aff = jax.nn.softmax(logits.reshape(n, N_EXPERTS), axis=-1)
    gate, idx = lax.top_k(aff.T, cap)
    idx_s, gate_s = lax.sort_key_val(idx, gate, dimension=1)
    ye = expert_ffn(hf[idx_s], gate_s[..., None], p["w_e1"], p["w_e3"], p["w_e2"])
    return ye, idx_s


def _trunk_layer(x, cond, s_f0, s_b0, p):
    mod = matmul(jax.nn.silu(cond), p["w_ada"], precise=True) + p["b_ada"]
    sh1, sc1, g1, sh2, sc2, g2 = [a[:, None, :] for a in jnp.split(mod, 6, axis=-1)]
    proj = in_projection(x, sc1, sh1, p["w_in_perm"])
    fourier = fourier_mix(proj)
    y_f, bo_f, s_f = wkv_scan(proj, p, 0, s_f0, False)
    y_b, bo_b, s_b = wkv_scan(proj, p, 1, s_b0, True)
    x1, h2, logits = post_mix(y_f, y_b, bo_f, bo_b, proj, fourier, x, g1, sc2, sh2, p)
    ye, idx_s = _expert_choice_ffn(h2, logits, p)
    x2 = combine(ye, idx_s, x1, g2, p["ln2_g"], p["ln2_b"])
    return x2, s_f, s_b


def kernel(x_prompt, x_sample, state_fwd, state_bwd, c, c_ctx, w_ada, b_ada, w_in, conv_w, w_decay0, w_decay_up, a0, a_up, g_up, k_k, k_a, r_k, gn_g, gn_b, w_fo, w_ro, w_out, ln1_g, ln1_b, w_router, w_e1, w_e3, w_e2, ln2_g, ln2_b):
    n_ctx = x_prompt.shape[0]
    cond_ctx = jnp.broadcast_to(c_ctx, (n_ctx, D_MODEL))
    zero_state = jnp.zeros((n_ctx, N_HEADS, HEAD_DIM, HEAD_DIM), _F32)
    names = ("w_ada", "b_ada", "w_in", "conv_w", "w_decay0", "w_decay_up", "a0", "a_up", "g_up", "k_k", "k_a",
             "r_k", "gn_g", "gn_b", "w_fo", "w_ro", "w_out", "ln1_g", "ln1_b", "w_router", "w_e1", "w_e3",
             "w_e2", "ln2_g", "ln2_b")
    stacked = (w_ada, b_ada, w_in, conv_w, w_decay0, w_decay_up, a0, a_up, g_up, k_k, k_a, r_k, gn_g, gn_b,
               w_fo, w_ro, w_out, ln1_g, ln1_b, w_router, w_e1, w_e3, w_e2, ln2_g, ln2_b)
    xp, xs = x_prompt, x_sample
    new_f, new_b = [], []
    for l in range(DEPTH):
        p = {k: a[l] for k, a in zip(names, stacked)}
        p["w_in_perm"] = _permute_in_weight(p["w_in"]).astype(_BF16)
        xp, s_f, s_b = _trunk_layer(xp, cond_ctx, zero_state, zero_state, p)
        new_f.append(s_f)
        new_b.append(s_b)
        xs, _, _ = _trunk_layer(xs, c, state_fwd[:, l], state_bwd[:, l], p)
    return (xp, xs, jnp.stack(new_f, axis=1), jnp.stack(new_b, axis=1))
```

```python
import functools

import numpy as np
import jax
import jax.numpy as jnp
from jax import lax
from jax.experimental import pallas as pl
from jax.experimental.pallas import tpu as pltpu

D_MODEL = 1024
N_FOURIER_GROUPS = 4
FOURIER_GROUP_DIM = 128
D_FOURIER = N_FOURIER_GROUPS * FOURIER_GROUP_DIM
HEAD_DIM = 64
D_RWKV = D_MODEL
N_HEADS = D_RWKV // HEAD_DIM
LORA_W = 64
LORA_A = 64
LORA_G = 128
N_DIR = 2
SHORT_CONV = 3
GN_EPS = 64e-5
N_EXPERTS = 16
CAPACITY_FACTOR = 2
D_EXPERT = 2048
LN_EPS = 1e-5
DEPTH = 1
DEEPNORM_ALPHA = (2.0 * DEPTH) ** 0.25
IN_WIDTHS = (D_FOURIER, 3 * D_RWKV, N_DIR * LORA_W, N_DIR * LORA_A, LORA_G, D_MODEL, D_MODEL)
D_IN = sum(IN_WIDTHS)
SPLIT_POINTS = tuple(sum(IN_WIDTHS[:i + 1]) for i in range(len(IN_WIDTHS) - 1))

LANES = 128
CHUNK = HEAD_DIM
CHUNKS_PER_STEP = 16
PAIRS_PER_STEP = 4
VMEM_LIMIT = 48 * 1024 * 1024

_HI = lax.Precision.HIGHEST
_F32 = jnp.float32
_BF16 = jnp.bfloat16


def _dot(a, b, precise):
    if precise:
        return jnp.dot(a, b, precision=_HI, preferred_element_type=_F32)
    return jnp.dot(a.astype(_BF16), b.astype(_BF16), preferred_element_type=_F32)


def _mm_kernel(a_ref, b_ref, o_ref, *, precise):
    o_ref[...] = _dot(a_ref[...], b_ref[...], precise)


def matmul(a, b, *, precise=False, tm=512, tn=512):
    m, k = a.shape
    n = b.shape[1]
    tm = min(tm, m)
    tn = min(tn, n)
    return pl.pallas_call(
        functools.partial(_mm_kernel, precise=precise),
        grid=(pl.cdiv(m, tm), pl.cdiv(n, tn)),
        in_specs=[pl.BlockSpec((tm, k), lambda i, j: (i, 0)),
                  pl.BlockSpec((k, tn), lambda i, j: (0, j))],
        out_specs=pl.BlockSpec((tm, tn), lambda i, j: (i, j)),
        out_shape=jax.ShapeDtypeStruct((m, n), _F32),
        compiler_params=pltpu.CompilerParams(
            dimension_semantics=("parallel", "parallel"), vmem_limit_bytes=VMEM_LIMIT),
    )(a, b)


P_R, P_K, P_V = 0, D_RWKV, 2 * D_RWKV
P_GATE_F = 3 * D_RWKV
P_GATE_R = P_GATE_F + D_MODEL
P_UF = P_GATE_R + D_MODEL
P_WDN = P_UF + D_FOURIER
P_ADN = P_WDN + N_DIR * LORA_W
P_GDN = P_ADN + N_DIR * LORA_A
D_PROJ = P_GDN + LORA_G + 128
assert D_PROJ % 512 == 0 and N_DIR * LORA_W == LANES and N_DIR * LORA_A == LANES and LORA_G == LANES


def _permute_in_weight(w_in):
    s = (0,) + SPLIT_POINTS + (D_IN,)
    seg = [w_in[:, s[i]:s[i + 1]] for i in range(len(IN_WIDTHS))]
    pad = jnp.zeros((w_in.shape[0], D_PROJ - D_IN), w_in.dtype)
    return jnp.concatenate([seg[1], seg[5], seg[6], seg[0], seg[2], seg[3], seg[4], pad], axis=1)


def _inproj_kernel(x_ref, sc_ref, sh_ref, w_ref, o_ref, h_scr):
    nb, tm, d = x_ref.shape

    @pl.when(pl.program_id(2) == 0)
    def _():
        h = x_ref[...] * (1.0 + sc_ref[...]) + sh_ref[...]
        h_scr[...] = h.reshape(nb * tm, d).astype(_BF16)

    out = jnp.dot(h_scr[...], w_ref[...], preferred_element_type=_F32)
    o_ref[...] = out.reshape(nb, tm, out.shape[-1])


def in_projection(x, sc, sh, w, *, rows=1024, tn=512):
    b, t, d = x.shape
    n = w.shape[1]
    tm = min(rows, t)
    nb = max(1, min(rows // tm, b))
    while b % nb:
        nb -= 1
    assert t % tm == 0
    return pl.pallas_call(
        _inproj_kernel,
        grid=(b // nb, t // tm, n // tn),
        in_specs=[pl.BlockSpec((nb, tm, d), lambda i, j, k: (i, j, 0)),
                  pl.BlockSpec((nb, 1, d), lambda i, j, k: (i, 0, 0)),
                  pl.BlockSpec((nb, 1, d), lambda i, j, k: (i, 0, 0)),
                  pl.BlockSpec((d, tn), lambda i, j, k: (0, k))],
        out_specs=pl.BlockSpec((nb, tm, tn), lambda i, j, k: (i, j, k)),
        out_shape=jax.ShapeDtypeStruct((b, t, n), _F32),
        scratch_shapes=[pltpu.VMEM((nb * tm, d), _BF16)],
        compiler_params=pltpu.CompilerParams(
            dimension_semantics=("parallel", "parallel", "arbitrary"), vmem_limit_bytes=VMEM_LIMIT),
        name="in_projection",
    )(x, sc, sh, w)


FOURIER_UNROLL = 8


def _dft_cos_sin(n):
    idx = np.arange(n)
    ang = 2.0 * np.pi * ((idx[:, None] * idx[None, :]) % n) / n
    return np.cos(ang), np.sin(ang)


def _split(x):
    x_h = x.astype(_BF16)
    return x_h, (x - x_h.astype(_F32)).astype(_BF16)


def _dot3(a, b):
    dg = lambda x, y: jnp.dot(x, y, preferred_element_type=_F32)
    return dg(a[0], b[0]) + dg(a[0], b[1]) + dg(a[1], b[0])


def _fourier_dense_kernel(u_ref, wc_ref, wt_ref, o_ref):
    t, gd = u_ref.shape[1], FOURIER_GROUP_DIM
    z = _dot3(_split(u_ref[0]), _split(wc_ref[...]))
    m = _dot3(_split(wt_ref[...]), _split(z))
    o_ref[0] = m[:t, :gd] - m[t:, gd:]


def _fourier_factored_kernel(u_ref, wc_ref, w1_ref, twc_ref, tws_ref, w2_ref, o_ref, p_scr, q_scr, a_scr,
                             *, n1):
    gd = FOURIER_GROUP_DIM
    z = _dot3(_split(u_ref[0]), _split(wc_ref[...]))
    p_scr[...] = z[:, :gd]
    q_scr[...] = z[:, gd:]
    w1 = _split(w1_ref[...])
    w2 = _split(w2_ref[...])

    def stage1(step, carry):
        t2s = [step * FOURIER_UNROLL + u for u in range(FOURIER_UNROLL)]
        rows = [pl.ds(t2, n1, stride=n1) for t2 in t2s]
        ms = [_dot3(w1, _split(jnp.concatenate([p_scr[r, :], q_scr[r, :]], axis=1))) for r in rows]
        for t2, m in zip(t2s, ms):
            x1r = m[:n1, :gd] - m[n1:, gd:]
            x1i = -(m[:n1, gd:] + m[n1:, :gd])
            c = twc_ref[t2]
            s = tws_ref[t2]
            a_scr[pl.ds(t2, n1, stride=2 * n1), :] = x1r * c + x1i * s
            a_scr[pl.ds(n1 + t2, n1, stride=2 * n1), :] = x1i * c - x1r * s
        return carry

    lax.fori_loop(0, n1 // FOURIER_UNROLL, stage1, 0)

    def stage2(step, carry):
        t1s = [step * FOURIER_UNROLL + u for u in range(FOURIER_UNROLL)]
        outs = [_dot3(w2, _split(a_scr[pl.ds(pl.multiple_of(t1 * 2 * n1, 2 * n1), 2 * n1), :])) for t1 in t1s]
        for t1, o in zip(t1s, outs):
            o_ref[0, pl.ds(t1, n1, stride=n1), :] = o
        return carry

    lax.fori_loop(0, n1 // FOURIER_UNROLL, stage2, 0)


def fourier_mix(proj):
    b, t, _ = proj.shape
    gd = FOURIER_GROUP_DIM
    cc, sc = _dft_cos_sin(gd)
    w_c = jnp.asarray(np.concatenate([cc, sc], axis=1) / np.sqrt(gd), _F32)
    u_spec = pl.BlockSpec((1, t, gd), lambda i, g: (i, 0, P_UF // gd + g))
    o_spec = pl.BlockSpec((1, t, gd), lambda i, g: (i, 0, g))
    full = lambda a: pl.BlockSpec(a.shape, lambda i, g: (0,) * a.ndim)
    params = pltpu.CompilerParams(dimension_semantics=("parallel", "parallel"), vmem_limit_bytes=VMEM_LIMIT)
    out_shape = jax.ShapeDtypeStruct((b, t, D_FOURIER), _F32)
    if t <= 512:
        ct, st = _dft_cos_sin(t)
        w_t = jnp.asarray(np.concatenate([ct, st], axis=0) / np.sqrt(t), _F32)
        return pl.pallas_call(
            _fourier_dense_kernel, grid=(b, N_FOURIER_GROUPS),
            in_specs=[u_spec, full(w_c), full(w_t)], out_specs=o_spec, out_shape=out_shape,
            compiler_params=params, name="fourier_dense",
        )(proj, w_c, w_t)
    n1 = int(round(np.sqrt(t)))
    assert n1 * n1 == t and n1 % FOURIER_UNROLL == 0
    c1, s1 = _dft_cos_sin(n1)
    w1 = jnp.asarray(np.concatenate([c1, s1], axis=0) / np.sqrt(n1), _F32)
    w2 = jnp.asarray(np.concatenate([c1, s1], axis=1) / np.sqrt(n1), _F32)
    tw = 2.0 * np.pi * (np.arange(n1)[:, None] * np.arange(n1)[None, :]) / t
    lanes = np.ones((1, 1, gd))
    twc = jnp.asarray(np.cos(tw)[:, :, None] * lanes, _F32)
    tws = jnp.asarray(np.sin(tw)[:, :, None] * lanes, _F32)
    return pl.pallas_call(
        functools.partial(_fourier_factored_kernel, n1=n1), grid=(b, N_FOURIER_GROUPS),
        in_specs=[u_spec, full(w_c), full(w1), full(twc), full(tws), full(w2)],
        out_specs=o_spec, out_shape=out_shape,
        scratch_shapes=[pltpu.VMEM((t, gd), _F32), pltpu.VMEM((t, gd), _F32), pltpu.VMEM((2 * t, gd), _F32)],
        compiler_params=params, name="fourier_factored",
    )(proj, w_c, w1, twc, tws, w2)


N_LEVELS = int(np.log2(CHUNK))

_NN = (((1,), (0,)), ((), ()))
_NT = (((1,), (1,)), ((), ()))
_TN = (((0,), (0,)), ((), ()))

_C_INCL, _C_STRICT, _C_EYE, _C_LEVEL0 = 0, 1, 2, 3
_C_LEFT = _C_LEVEL0 + N_LEVELS
_C_RIGHT = _C_LEFT + 1


def _scan_consts(reverse):
    L = CHUNK
    t = np.arange(L)[:, None]
    s = np.arange(L)[None, :]
    if reverse:
        t, s = s, t
    strict = (s < t)
    incl = (s <= t)
    levels = []
    bs = 1
    while bs < L:
        levels.append((t // (2 * bs) == s // (2 * bs)) & (t % (2 * bs) >= bs) & (s % (2 * bs) < bs))
        bs *= 2
    pair = lambda m: np.concatenate([m, m], axis=1)
    left = np.concatenate([np.ones((L, HEAD_DIM)), np.zeros((L, HEAD_DIM))], axis=1)
    slabs = [pair(incl), pair(strict), pair(np.eye(L))] + [pair(l) for l in levels] + [left, 1.0 - left]
    return np.stack(slabs).astype(np.float32)


def _bdot(a, b, dims):
    return lax.dot_general(a, b, dims, preferred_element_type=_F32)


def _wkv_kernel(c_ref, r_ref, k_ref, v_ref, rp_ref, kp_ref, vp_ref, rn_ref, kn_ref, vn_ref,
                cwr_ref, cwk_ref, cwv_ref, wdn_ref, adn_ref, wup_ref, aup_ref, w0_ref, a0_ref,
                kkw_ref, kaw_ref, rkw_ref, s0_ref, y_ref, bo_ref, st_ref, s_scr, *, nsub, npp, reverse):
    L, N = CHUNK, HEAD_DIM
    tb = nsub * L
    j = pl.program_id(2)
    nblk = pl.num_programs(2)
    bf = lambda x: x.astype(_BF16)
    inclp = c_ref[_C_INCL]
    left = c_ref[_C_LEFT]
    right = c_ref[_C_RIGHT]
    left_b, right_b = bf(left), bf(right)
    bdmask = jnp.concatenate([left, right], axis=0)
    bdmask_b = bf(bdmask)
    incl_b = bf(inclp[:, :L])
    strict_b = bf(c_ref[_C_STRICT])
    incl2_b = bf(jnp.concatenate([inclp, inclp], axis=1))
    level_b = [(bf(c_ref[_C_LEVEL0 + lv] * left), bf(c_ref[_C_LEVEL0 + lv] * right)) for lv in range(N_LEVELS)]

    tblk = (nblk - 1 - j) if reverse else j
    has_prev = (tblk > 0).astype(_F32)
    has_next = (tblk < nblk - 1).astype(_F32)
    row = lax.broadcasted_iota(jnp.int32, (tb, 1), 0)

    def conv(u_ref, up_ref, un_ref, cw_ref):
        u = u_ref[0]
        before = jnp.where(row == 0, up_ref[0, 7:8, :] * has_prev, pltpu.roll(u, 1, 0))
        after = jnp.where(row == tb - 1, un_ref[0, 0:1, :] * has_next, pltpu.roll(u, tb - 1, 0))
        return before * cw_ref[0:1, :] + u * cw_ref[1:2, :] + after * cw_ref[2:3, :]

    def segsum(x):
        x_h = x.astype(_BF16)
        x_l = (x - x_h.astype(_F32)).astype(_BF16)
        return (jnp.dot(x_h, bdmask_b, preferred_element_type=_F32)
                + jnp.dot(x_l, bdmask_b, preferred_element_type=_F32))

    r_all = conv(r_ref, rp_ref, rn_ref, cwr_ref)
    k_all = conv(k_ref, kp_ref, kn_ref, cwk_ref)
    v_all = conv(v_ref, vp_ref, vn_ref, cwv_ref)
    w_up = jnp.dot(jnp.tanh(wdn_ref[0]).astype(_BF16), wup_ref[...].astype(_BF16), preferred_element_type=_F32)
    a_up = jnp.dot(adn_ref[0].astype(_BF16), aup_ref[...].astype(_BF16), preferred_element_type=_F32)
    z = -(w0_ref[...] + w_up)
    softplus = jnp.maximum(z, 0.0) + jnp.log(1.0 + jnp.exp(-jnp.abs(z)))
    lw_all = -jnp.exp(-softplus - 0.5)
    a_rate = jax.nn.sigmoid(a0_ref[...] + a_up)
    kd_all = k_all * (1.0 + (a_rate - 1.0) * kaw_ref[...])
    kkr = k_all * kkw_ref[...]
    rkd = r_all * kd_all * rkw_ref[...]
    kk_parts, bonus_parts = [], []
    for q in range(npp):
        sl = slice(q * LANES, (q + 1) * LANES)
        kq = kkr[:, sl]
        kk_parts.append(kq * lax.rsqrt(segsum(kq * kq) + 1e-12))
        bonus_parts.append(segsum(rkd[:, sl]) * v_all[:, sl])
    bo_ref[0] = jnp.concatenate(bonus_parts, axis=1)
    kk_all = jnp.concatenate(kk_parts, axis=1)
    bb_all = kk_all * a_rate

    def bd(x):
        return jnp.concatenate([x * left_b, x * right_b], axis=0)

    @pl.when(j == 0)
    def _():
        zero = jnp.zeros((N, N), _F32)
        for q in range(npp):
            top = jnp.concatenate([s0_ref[0, 2 * q], zero], axis=1)
            bot = jnp.concatenate([zero, s0_ref[0, 2 * q + 1]], axis=1)
            s_scr[q] = jnp.concatenate([top, bot], axis=0)

    order = range(nsub - 1, -1, -1) if reverse else range(nsub)
    last_row = 0 if reverse else L - 1

    chains = [(q, i) for i in order for q in range(npp)]

    def each(fn, *lists):
        return [fn(*args) for args in zip(*lists)]

    def tile(x, c):
        q, i = c
        return x[i * L:(i + 1) * L, q * LANES:(q + 1) * LANES]

    lw = [tile(lw_all, c) for c in chains]

    def cumulative(x):
        x_h = x.astype(_BF16)
        rem = x - x_h.astype(_F32)
        x_m = rem.astype(_BF16)
        x_l = (rem - x_m.astype(_F32)).astype(_BF16)
        return (jnp.dot(incl_b, x_h, preferred_element_type=_F32)
                + jnp.dot(incl_b, x_m, preferred_element_type=_F32)
                + jnp.dot(incl_b, x_l, preferred_element_type=_F32))

    cum = each(cumulative, lw)
    tot = [x[last_row:last_row + 1, :] for x in cum]
    g_iv = [jnp.exp(-x) for x in cum]
    g_rem = each(lambda t_, x: jnp.exp(t_ - x), tot, cum)
    v = [bf(tile(v_all, c)) for c in chains]
    rt = [tile(r_all, c) * jnp.exp(x) for c, x in zip(chains, cum)]
    at = [bf(-tile(kk_all, c) * jnp.exp(x - w)) for c, x, w in zip(chains, cum, lw)]
    kd = [tile(kd_all, c) for c in chains]
    bb = [tile(bb_all, c) for c in chains]
    bh = each(lambda x, g: bf(x * g), bb, g_rem)
    kh = each(lambda x, g: bf(x * g), kd, g_rem)
    gram = each(lambda a, r_, b_, k_, g: _bdot(jnp.concatenate([a, bf(r_)], axis=0),
                                               jnp.concatenate([bd(bf(b_ * g)), bd(bf(k_ * g))], axis=0), _NT),
                at, rt, bb, kd, g_iv)
    a_ab = [bf(g[:L, :LANES]) for g in gram]
    a_ak = [bf(g[:L, LANES:]) * strict_b for g in gram]
    a_rbk = [bf(g[L:, :]) * incl2_b for g in gram]
    tinv = [c_ref[_C_EYE] + g[:L, :LANES] * c_ref[_C_LEVEL0] for g in gram]
    for lv in range(1, N_LEVELS):
        m_left, m_right = level_b[lv]
        tinv_b = [bf(t_) for t_ in tinv]
        tx = each(lambda t_, a: _bdot(t_, jnp.concatenate([a * m_left, a * m_right], axis=0), _NN), tinv_b, a_ab)
        tinv = each(lambda t_, tb_, x: t_ + _bdot(bf(x), bd(tb_), _NN), tinv, tinv_b, tx)
    av = each(lambda a, v_: _bdot(a, bd(v_), _NN), a_ak, v)
    tz = each(lambda t_, a, x: bf(_bdot(bf(t_), jnp.concatenate([bd(a), bd(bf(x))], axis=1), _NN)), tinv, at, av)
    wt = [x[:, :LANES] for x in tz]
    u0 = [x[:, LANES:] for x in tz]
    zero_bd = jnp.zeros((LANES, LANES), _BF16)
    qy = each(lambda a, w, u, v_: _bdot(
        a, jnp.concatenate([jnp.concatenate([bd(w), bd(u)], axis=1),
                            jnp.concatenate([zero_bd, bd(v_)], axis=1)], axis=0), _NN),
        a_rbk, wt, u0, v)
    qq = each(lambda r_, x: bf(r_ + x[:, :LANES]), rt, qy)
    y0 = [x[:, LANES:] for x in qy]
    pm = each(lambda w, b_: bf(_bdot(w, b_, _TN)) * bdmask_b, wt, bh)
    cp = each(lambda u, v_, b_, k_: _bdot(jnp.concatenate([u, v_], axis=0), jnp.concatenate([b_, k_], axis=0),
                                          _TN) * bdmask, u0, v, bh, kh)
    gl = [jnp.exp(x) for x in tot]

    for n, (q, i) in enumerate(chains):
        s0 = s_scr[q]
        s0_b = bf(s0)
        y_ref[0, i * L:(i + 1) * L, q * LANES:(q + 1) * LANES] = _bdot(qq[n], s0_b, _NT) + y0[n]
        s_scr[q] = s0 * gl[n] + _bdot(s0_b, pm[n], _NN) + cp[n]

    @pl.when(j == pl.num_programs(2) - 1)
    def _():
        for q in range(npp):
            s = s_scr[q]
            st_ref[0, 2 * q] = s[:N, :N]
            st_ref[0, 2 * q + 1] = s[N:, N:]


def wkv_scan(proj, p, di, s0, reverse):
    b, t, _ = proj.shape
    d = D_RWKV
    nsub = min(CHUNKS_PER_STEP, t // CHUNK)
    npp = PAIRS_PER_STEP
    tb = nsub * CHUNK
    width = LANES * npp
    nblk = t // tb
    ngrp = d // width
    halo = 8
    consts = jnp.asarray(_scan_consts(reverse))
    pad_rows = lambda w: jnp.zeros((LANES, d), _F32).at[di * w.shape[0]:(di + 1) * w.shape[0]].set(w)
    wup = pad_rows(p["w_decay_up"][di])
    aup = pad_rows(p["a_up"][di])
    row = lambda a: a.reshape(1, d)

    tmap = (lambda j: nblk - 1 - j) if reverse else (lambda j: j)

    def seq(col0):
        return pl.BlockSpec((1, tb, width), lambda i, g, j: (i, tmap(j), col0 // width + g))

    def prev(col0):
        return pl.BlockSpec((1, halo, width),
                            lambda i, g, j: (i, jnp.maximum(tmap(j) * (tb // halo) - 1, 0), col0 // width + g))

    def nxt(col0):
        return pl.BlockSpec((1, halo, width),
                            lambda i, g, j: (i, jnp.minimum((tmap(j) + 1) * (tb // halo), t // halo - 1),
                                             col0 // width + g))

    def lowrank(col0):
        return pl.BlockSpec((1, tb, LANES), lambda i, g, j: (i, tmap(j), col0 // LANES))

    def cols(nrows, col0=0):
        return pl.BlockSpec((nrows, width), lambda i, g, j: (0, col0 // width + g))

    out_seq = pl.BlockSpec((1, tb, width), lambda i, g, j: (i, tmap(j), g))
    st_spec = pl.BlockSpec((1, 2 * npp, HEAD_DIM, HEAD_DIM), lambda i, g, j: (i, g, 0, 0))
    in_specs = ([pl.BlockSpec(consts.shape, lambda i, g, j: (0, 0, 0))]
                + [seq(P_R), seq(P_K), seq(P_V), prev(P_R), prev(P_K), prev(P_V), nxt(P_R), nxt(P_K), nxt(P_V)]
                + [cols(SHORT_CONV, 0), cols(SHORT_CONV, d), cols(SHORT_CONV, 2 * d)]
                + [lowrank(P_WDN), lowrank(P_ADN), cols(LANES), cols(LANES)]
                + [cols(1)] * 5 + [st_spec])
    y, bonus, st = pl.pallas_call(
        functools.partial(_wkv_kernel, nsub=nsub, npp=npp, reverse=reverse),
        grid=(b, ngrp, nblk),
        in_specs=in_specs,
        out_specs=[out_seq, out_seq, st_spec],
        out_shape=[jax.ShapeDtypeStruct((b, t, d), _F32),
                   jax.ShapeDtypeStruct((b, t, d), _F32),
                   jax.ShapeDtypeStruct(s0.shape, _F32)],
        scratch_shapes=[pltpu.VMEM((npp, LANES, LANES), _F32)],
        compiler_params=pltpu.CompilerParams(
            dimension_semantics=("parallel", "parallel", "arbitrary"), vmem_limit_bytes=VMEM_LIMIT),
        name="wkv_scan_bwd" if reverse else "wkv_scan_fwd",
    )(consts, proj, proj, proj, proj, proj, proj, proj, proj, proj,
      p["conv_w"], p["conv_w"], p["conv_w"], proj, proj, wup, aup,
      row(p["w_decay0"][di]), row(p["a0"][di]), row(p["k_k"]), row(p["k_a"]), row(p["r_k"]), s0)
    return y, bonus, st


def _layer_norm_rows(x, g, b):
    mu = jnp.mean(x, axis=-1, keepdims=True)
    xc = x - mu
    var = jnp.mean(xc * xc, axis=-1, keepdims=True)
    return xc * lax.rsqrt(var + LN_EPS) * g + b


def _post_kernel(yf_ref, yb_ref, bf_ref, bb_ref, gdn_ref, gf_ref, gr_ref, fo_ref, x_ref,
                 g1_ref, sc2_ref, sh2_ref, gup_ref, gng_ref, gnb_ref, wfo_ref, wro_ref, wout_ref,
                 l1g_ref, l1b_ref, wr_ref, bd_ref, x1_ref, h2_ref, lg_ref):
    bdm = bd_ref[...]

    def segsum(x):
        x_h = x.astype(_BF16)
        x_l = (x - x_h.astype(_F32)).astype(_BF16)
        return jnp.dot(x_h, bdm, preferred_element_type=_F32) + jnp.dot(x_l, bdm, preferred_element_type=_F32)

    y = yf_ref[0] + yb_ref[0]
    parts = []
    for q in range(D_RWKV // LANES):
        yq = y[:, q * LANES:(q + 1) * LANES]
        dq = yq - segsum(yq) * (1.0 / HEAD_DIM)
        parts.append(dq * lax.rsqrt(segsum(dq * dq) * (1.0 / HEAD_DIM) + GN_EPS))
    yn = jnp.concatenate(parts, axis=1) * gng_ref[...] + gnb_ref[...]
    g = jnp.dot(jax.nn.sigmoid(gdn_ref[0]).astype(_BF16), gup_ref[...], preferred_element_type=_F32)
    z = (yn + bf_ref[0] + bb_ref[0]) * g
    o_r = jnp.dot(z.astype(_BF16), wro_ref[...], preferred_element_type=_F32)
    o_f = jnp.dot(fo_ref[0].astype(_BF16), wfo_ref[...], preferred_element_type=_F32)
    merged = jax.nn.sigmoid(gf_ref[0]) * o_f + jax.nn.sigmoid(gr_ref[0]) * o_r
    mix = jnp.dot(merged.astype(_BF16), wout_ref[...], preferred_element_type=_F32)
    x1 = _layer_norm_rows(DEEPNORM_ALPHA * x_ref[0] + g1_ref[0] * mix, l1g_ref[...], l1b_ref[...])
    x1_ref[0] = x1
    h2 = x1 * (1.0 + sc2_ref[0]) + sh2_ref[0]
    h2_ref[0] = h2.astype(_BF16)
    lg_ref[0] = jnp.dot(h2, wr_ref[...], precision=_HI, preferred_element_type=_F32)


def post_mix(y_f, y_b, bo_f, bo_b, proj, fourier, x, g1, sc2, sh2, p, *, tm=256):
    b, t, d = x.shape
    tm = min(tm, t)
    bdm = jnp.asarray(np.kron(np.eye(2), np.ones((HEAD_DIM, HEAD_DIM))), _BF16)
    tok = lambda w, c0=0: pl.BlockSpec((1, tm, w), lambda i, j: (i, j, c0 // w))
    per_seq = pl.BlockSpec((1, 1, d), lambda i, j: (i, 0, 0))
    full = lambda a: pl.BlockSpec(a.shape, lambda i, j: (0,) * a.ndim)
    row = lambda a: a.reshape(1, -1)
    weights = [p["g_up"].astype(_BF16), row(p["gn_g"]), row(p["gn_b"]), p["w_fo"].astype(_BF16),
               p["w_ro"].astype(_BF16), p["w_out"].astype(_BF16), row(p["ln1_g"]), row(p["ln1_b"]),
               p["w_router"], bdm]
    return pl.pallas_call(
        _post_kernel,
        grid=(b, t // tm),
        in_specs=[tok(d), tok(d), tok(d), tok(d), tok(LORA_G, P_GDN), tok(d, P_GATE_F), tok(d, P_GATE_R),
                  tok(D_FOURIER), tok(d), per_seq, per_seq, per_seq] + [full(w) for w in weights],
        out_specs=[tok(d), tok(d), tok(N_EXPERTS)],
        out_shape=[jax.ShapeDtypeStruct((b, t, d), _F32),
                   jax.ShapeDtypeStruct((b, t, d), _BF16),
                   jax.ShapeDtypeStruct((b, t, N_EXPERTS), _F32)],
        compiler_params=pltpu.CompilerParams(
            dimension_semantics=("parallel", "parallel"), vmem_limit_bytes=VMEM_LIMIT),
        name="post_mix",
    )(y_f, y_b, bo_f, bo_b, proj, proj, proj, fourier, x, g1, sc2, sh2, *weights)


def _expert_kernel(x_ref, g_ref, w1_ref, w3_ref, w2_ref, o_ref, acc_ref):
    f = pl.program_id(2)
    x = x_ref[0].astype(_BF16)
    h1 = jnp.dot(x, w1_ref[0].astype(_BF16), preferred_element_type=_F32)
    h3 = jnp.dot(x, w3_ref[0].astype(_BF16), preferred_element_type=_F32)
    he = (h1 * jax.nn.sigmoid(h1)) * h3
    part = jnp.dot(he.astype(_BF16), w2_ref[0].astype(_BF16), preferred_element_type=_F32)

    @pl.when(f == 0)
    def _():
        acc_ref[...] = part

    @pl.when(f > 0)
    def _():
        acc_ref[...] += part

    @pl.when(f == pl.num_programs(2) - 1)
    def _():
        o_ref[0] = (acc_ref[...] * g_ref[0]).astype(o_ref.dtype)


def expert_ffn(xe, gate, w1, w3, w2, *, tm=1024, tf=512):
    e, c, d = xe.shape
    f = w1.shape[2]
    tm = min(tm, c)
    return pl.pallas_call(
        _expert_kernel,
        grid=(e, c // tm, f // tf),
        in_specs=[pl.BlockSpec((1, tm, d), lambda i, j, k: (i, j, 0)),
                  pl.BlockSpec((1, tm, 1), lambda i, j, k: (i, j, 0)),
                  pl.BlockSpec((1, d, tf), lambda i, j, k: (i, 0, k)),
                  pl.BlockSpec((1, d, tf), lambda i, j, k: (i, 0, k)),
                  pl.BlockSpec((1, tf, d), lambda i, j, k: (i, k, 0))],
        out_specs=pl.BlockSpec((1, tm, d), lambda i, j, k: (i, j, 0)),
        out_shape=jax.ShapeDtypeStruct((e, c, d), _BF16),
        scratch_shapes=[pltpu.VMEM((tm, d), _F32)],
        compiler_params=pltpu.CompilerParams(
            dimension_semantics=("parallel", "parallel", "arbitrary"), vmem_limit_bytes=VMEM_LIMIT),
        name="expert_ffn",
    )(xe, gate, w1, w3, w2)


COMBINE_WINDOW = 256
COMBINE_TOKENS = 512


def _combine_kernel(starts_ref, idx_ref, ye_hbm, x1_ref, g2_ref, lg_ref, lb_ref, o_ref, buf, acc, sem,
                    *, tm, cap, win):
    j = pl.program_id(0)
    n_tiles = pl.num_programs(0)
    n_exp = idx_ref.shape[0]
    spare = n_exp
    tok = j * tm + lax.broadcasted_iota(jnp.int32, (tm, 1), 0)

    def first_row(e, tile):
        return (starts_ref[e, tile] // LANES) * LANES

    def window_start(e, k, tile):
        return pl.multiple_of(jnp.minimum(first_row(e, tile) + k * win, cap - win), LANES)

    def n_windows(e):
        return jnp.maximum((starts_ref[e, j + 1] - first_row(e, j) + win - 1) // win, 1)

    def copy(e, k, tile, slot):
        rows = pl.ds(window_start(e, k, tile), win)
        return pltpu.make_async_copy(ye_hbm.at[e, rows, :], buf.at[slot], sem.at[slot])

    def place(e, k, slot):
        ids = idx_ref[e:e + 1, pl.ds(window_start(e, k, j), win)]
        row_of = window_start(e, k, j) + lax.broadcasted_iota(jnp.int32, (1, win), 1)
        ids = jnp.where(row_of >= first_row(e, j) + k * win, ids, -1)
        onehot = (tok == ids).astype(_BF16)
        acc[...] += jnp.dot(onehot, buf[slot], preferred_element_type=_F32)

    @pl.when(j == 0)
    def _():
        for e in range(n_exp):
            copy(e, 0, 0, e).start()

    acc[...] = jnp.zeros_like(acc)
    for e in range(n_exp):
        copy(e, 0, j, e).wait()
        place(e, 0, e)

        @pl.when(j + 1 < n_tiles)
        def _(e=e):
            copy(e, 0, j + 1, e).start()

        def further(k, carry, e=e):
            cp = copy(e, k, j, spare)
            cp.start()
            cp.wait()
            place(e, k, spare)
            return carry

        lax.fori_loop(1, n_windows(e), further, 0)

    x = DEEPNORM_ALPHA * x1_ref[...] + g2_ref[0] * acc[...]
    o_ref[...] = _layer_norm_rows(x, lg_ref[...], lb_ref[...])


def combine(ye, idx_s, x1, g2, ln_g, ln_b):
    n_exp, cap, d = ye.shape
    b, t, _ = x1.shape
    n = b * t
    tm = min(COMBINE_TOKENS, t)
    win = min(COMBINE_WINDOW, cap)
    ntiles = n // tm
    bounds = jnp.arange(ntiles + 1, dtype=jnp.int32) * tm
    starts = jnp.sum(idx_s[:, :, None] < bounds[None, None, :], axis=1).astype(jnp.int32)
    row = lambda a: a.reshape(1, d)
    grid_spec = pltpu.PrefetchScalarGridSpec(
        num_scalar_prefetch=1,
        grid=(ntiles,),
        in_specs=[pl.BlockSpec((n_exp, cap), lambda j, st: (0, 0)),
                  pl.BlockSpec(memory_space=pl.ANY),
                  pl.BlockSpec((tm, d), lambda j, st: (j, 0)),
                  pl.BlockSpec((1, 1, d), lambda j, st: (j * tm // t, 0, 0)),
                  pl.BlockSpec((1, d), lambda j, st: (0, 0)),
                  pl.BlockSpec((1, d), lambda j, st: (0, 0))],
        out_specs=pl.BlockSpec((tm, d), lambda j, st: (j, 0)),
        scratch_shapes=[pltpu.VMEM((n_exp + 1, win, d), ye.dtype), pltpu.VMEM((tm, d), _F32),
                        pltpu.SemaphoreType.DMA((n_exp + 1,))],
    )
    out = pl.pallas_call(
        functools.partial(_combine_kernel, tm=tm, cap=cap, win=win),
        grid_spec=grid_spec,
        out_shape=jax.ShapeDtypeStruct((n, d), _F32),
        compiler_params=pltpu.CompilerParams(dimension_semantics=("arbitrary",), vmem_limit_bytes=VMEM_LIMIT),
        name="combine",
    )(starts, idx_s, ye, x1.reshape(n, d), g2, row(ln_g), row(ln_b))
    return out.reshape(b, t, d)


def _expert_choice_ffn(h2, logits, p):
    b, t, d = h2.shape
    n = b * t
    cap = CAPACITY_FACTOR * n // N_EXPERTS
    hf = h2.reshape(n, d)
    aff = jax.nn.softmax(logits.reshape(n, N_EXPERTS), axis=-1)
    gate, idx = lax.top_k(aff.T, cap)
    idx_s, gate_s = lax.sort_key_val(idx, gate, dimension=1)
    ye = expert_ffn(hf[idx_s], gate_s[..., None], p["w_e1"], p["w_e3"], p["w_e2"])
    return ye, idx_s


def _trunk_layer(x, cond, s_f0, s_b0, p):
    mod = matmul(jax.nn.silu(cond), p["w_ada"], precise=True) + p["b_ada"]
    sh1, sc1, g1, sh2, sc2, g2 = [a[:, None, :] for a in jnp.split(mod, 6, axis=-1)]
    proj = in_projection(x, sc1, sh1, p["w_in_perm"])
    fourier = fourier_mix(proj)
    y_f, bo_f, s_f = wkv_scan(proj, p, 0, s_f0, False)
    y_b, bo_b, s_b = wkv_scan(proj, p, 1, s_b0, True)
    x1, h2, logits = post_mix(y_f, y_b, bo_f, bo_b, proj, fourier, x, g1, sc2, sh2, p)
    ye, idx_s = _expert_choice_ffn(h2, logits, p)
    x2 = combine(ye, idx_s, x1, g2, p["ln2_g"], p["ln2_b"])
    return x2, s_f, s_b


def kernel(x_prompt, x_sample, state_fwd, state_bwd, c, c_ctx, w_ada, b_ada, w_in, conv_w, w_decay0, w_decay_up, a0, a_up, g_up, k_k, k_a, r_k, gn_g, gn_b, w_fo, w_ro, w_out, ln1_g, ln1_b, w_router, w_e1, w_e3, w_e2, ln2_g, ln2_b):
    n_ctx = x_prompt.shape[0]
    cond_ctx = jnp.broadcast_to(c_ctx, (n_ctx, D_MODEL))
    zero_state = jnp.zeros((n_ctx, N_HEADS, HEAD_DIM, HEAD_DIM), _F32)
    names = ("w_ada", "b_ada", "w_in", "conv_w", "w_decay0", "w_decay_up", "a0", "a_up", "g_up", "k_k", "k_a",
             "r_k", "gn_g", "gn_b", "w_fo", "w_ro", "w_out", "ln1_g", "ln1_b", "w_router", "w_e1", "w_e3",
             "w_e2", "ln2_g", "ln2_b")
    stacked = (w_ada, b_ada, w_in, conv_w, w_decay0, w_decay_up, a0, a_up, g_up, k_k, k_a, r_k, gn_g, gn_b,
               w_fo, w_ro, w_out, ln1_g, ln1_b, w_router, w_e1, w_e3, w_e2, ln2_g, ln2_b)
    xp, xs = x_prompt, x_sample
    new_f, new_b = [], []
    for l in range(DEPTH):
        p = {k: a[l] for k, a in zip(names, stacked)}
        p["w_in_perm"] = _permute_in_weight(p["w_in"]).astype(_BF16)
        xp, s_f, s_b = _trunk_layer(xp, cond_ctx, zero_state, zero_state, p)
        new_f.append(s_f)
        new_b.append(s_b)
        xs, _, _ = _trunk_layer(xs, c, state_fwd[:, l], state_bwd[:, l], p)
    return (xp, xs, jnp.stack(new_f, axis=1), jnp.stack(new_b, axis=1))
```
